```python
import functools
import jax, jax.numpy as jnp
from jax import lax
import numpy as np

D_MODEL = 2048
BATCH = 2
SEQ = 4096
DEPTH = 2
DEC_BATCH = 32
DEC_SEQ = 4
PAST_LEN = 8192
PAGE_SIZE = 128

D_A = D_MODEL // 2
K_A = 3
D_B = D_MODEL // 2
K_B = 31
HD_C = 128
D_C = D_MODEL // 2
H_C = D_C // HD_C
H_M = 4
D_M = D_MODEL // 2
HD_M = D_M // H_M
N_MEM = 256
N_BRANCH = 4
D_FF = ((8 * D_MODEL // 3 + 255) // 256) * 256
Q_BLOCK = 128
IN_SIZES = (D_A, D_A, D_A, D_B, D_B, D_C, D_C, D_C, H_C, D_M, N_BRANCH * D_MODEL)
N_IN = sum(IN_SIZES)
NORM_EPS = 1e-6
LN_EPS = 1e-5
F_BIAS = 8.0

kernel_name = "hybrid_conv_fox_memory_decoder_step"


def _rms(x, g):
    xf = x.astype(jnp.float32)
    y = xf * lax.rsqrt(jnp.mean(jnp.square(xf), axis=-1, keepdims=True) + NORM_EPS)
    return (y * g.astype(jnp.float32)).astype(x.dtype)


def _layernorm(x, g, b):
    xf = x.astype(jnp.float32)
    mu = jnp.mean(xf, axis=-1, keepdims=True)
    var = jnp.mean(jnp.square(xf - mu), axis=-1, keepdims=True)
    y = (xf - mu) * lax.rsqrt(var + LN_EPS)
    return (y * g.astype(jnp.float32) + b.astype(jnp.float32)).astype(x.dtype)


def _swiglu(x, wg, wu, wd):
    return (jax.nn.silu(x @ wg) * (x @ wu)) @ wd


def _causal_dwconv(x_ext, w):
    return lax.conv_general_dilated(
        x_ext, w[:, None, :].astype(x_ext.dtype), window_strides=(1,), padding='VALID',
        dimension_numbers=('NWC', 'WIO', 'NWC'), feature_group_count=w.shape[1])


def _split_in(u):
    points, acc = [], 0
    for s in IN_SIZES[:-1]:
        acc += s
        points.append(acc)
    return jnp.split(u, points, axis=-1)


def _mem_kv(mem, n_mem, w_mk, w_mv, kn_m):
    b, n, _ = mem.shape
    m = _rms(mem, n_mem)
    mk = _rms((m @ w_mk).reshape(b, n, H_M, HD_M), kn_m)
    mv = (m @ w_mv).reshape(b, n, H_M, HD_M)
    return mk, mv


def _mem_attend(q, mk, mv):
    s = jnp.einsum('bqhd,bnhd->bhqn', q, mk).astype(jnp.float32) * (HD_M ** -0.5)
    p = jax.nn.softmax(s, axis=-1).astype(mv.dtype)
    return jnp.einsum('bhqn,bnhd->bqhd', p, mv)


def _fox_prompt(q, k, v, lf):
    b, t, h, dh = q.shape
    nb = t // Q_BLOCK
    c_k = jnp.swapaxes(jnp.cumsum(lf, axis=1), 1, 2)
    pos_k = jnp.arange(t)
    qb = jnp.moveaxis(q.reshape(b, nb, Q_BLOCK, h, dh), 1, 0)
    cb = jnp.moveaxis(c_k.reshape(b, h, nb, Q_BLOCK), 2, 0)
    scale = HD_C ** -0.5

    def block(args):
        i, q_i, cq_i = args
        pos_q = i * Q_BLOCK + jnp.arange(Q_BLOCK)
        s = jnp.einsum('bqhd,bkhd->bhqk', q_i, k).astype(jnp.float32) * scale
        s = s + (cq_i[..., :, None] - c_k[..., None, :])
        s = jnp.where(pos_k[None, :] <= pos_q[:, None], s, -jnp.inf)
        pr = jax.nn.softmax(s, axis=-1).astype(v.dtype)
        return jnp.einsum('bhqk,bkhd->bqhd', pr, v)

    o = lax.map(block, (jnp.arange(nb), qb, cb))
    return jnp.moveaxis(o, 0, 1).reshape(b, t, h, dh)


def _fox_sample(q, k, v, lf, k_past, v_past, lf_past):
    t = q.shape[1]
    n_past = k_past.shape[1]
    scale = HD_C ** -0.5
    lfp = lf_past.astype(jnp.float32)
    c_past = lfp - jnp.cumsum(lfp[:, ::-1], axis=1)[:, ::-1]
    c_new = jnp.cumsum(lf, axis=1)
    cq = jnp.swapaxes(c_new, 1, 2)[..., :, None]
    s_p = jnp.einsum('bqhd,bkhd->bhqk', q, k_past).astype(jnp.float32) * scale
    s_p = s_p + (cq - jnp.swapaxes(c_past, 1, 2)[..., None, :])
    s_n = jnp.einsum('bqhd,bkhd->bhqk', q, k).astype(jnp.float32) * scale
    s_n = s_n + (cq - jnp.swapaxes(c_new, 1, 2)[..., None, :])
    s_n = jnp.where(jnp.tril(jnp.ones((t, t), bool)), s_n, -jnp.inf)
    pr = jax.nn.softmax(jnp.concatenate([s_p, s_n], axis=-1), axis=-1).astype(v.dtype)
    return (jnp.einsum('bhqk,bkhd->bqhd', pr[..., :n_past], v_past)
            + jnp.einsum('bhqk,bkhd->bqhd', pr[..., n_past:], v))


def _layer(x, p, hist_a, hist_b, fox_fn, mem_k, mem_v):
    b, t, _ = x.shape
    x = x + 0.5 * _swiglu(_rms(x, p['n_ffn1']), p['w1g'], p['w1u'], p['w1d'])
    h = _rms(x, p['n_mix'])
    (b_a, c_a, x_a, glu_1, glu_2, q_c, k_c, v_c, f_c, q_m, gates) = _split_in(h @ p['w_in'])
    ext_a = jnp.concatenate([hist_a, c_a * x_a], axis=1)
    y_a = (b_a * _causal_dwconv(ext_a, p['conv_a'])) @ p['w_a_out']
    ext_b = jnp.concatenate([hist_b, glu_1 * jax.nn.sigmoid(glu_2)], axis=1)
    z_b = _layernorm(_causal_dwconv(ext_b, p['conv_b']), p['ln_b_g'], p['ln_b_b'])
    y_b = jax.nn.silu(z_b) @ p['w_b_out']
    q = _rms(q_c.reshape(b, t, H_C, HD_C), p['qn_c'])
    k = _rms(k_c.reshape(b, t, H_C, HD_C), p['kn_c'])
    v = v_c.reshape(b, t, H_C, HD_C)
    lf = jax.nn.log_sigmoid(f_c.astype(jnp.float32) + p['b_f'].astype(jnp.float32))
    y_c = fox_fn(q, k, v, lf).reshape(b, t, D_C) @ p['w_c_out']
    qm = _rms(q_m.reshape(b, t, H_M, HD_M), p['qn_m'])
    y_m = _mem_attend(qm, mem_k, mem_v).reshape(b, t, D_M) @ p['w_m_out']
    g = jax.nn.sigmoid(gates).reshape(b, t, N_BRANCH, D_MODEL)
    merged = g[:, :, 0] * y_a + g[:, :, 1] * y_b + g[:, :, 2] * y_c + g[:, :, 3] * y_m
    x = x + merged @ p['w_o']
    x = x + 0.5 * _swiglu(_rms(x, p['n_ffn2']), p['w2g'], p['w2u'], p['w2d'])
    return x, ext_a[:, -(K_A - 1):], ext_b[:, -(K_B - 1):], k, v, lf


def setup_inputs(seed: int = 0) -> dict:
    key = jax.random.key(seed)
    ks = iter(jax.random.split(key, 64))

    def nrm(shape, scale):
        return jax.random.normal(next(ks), shape, jnp.float32) * scale

    def gain(shape):
        return 1.0 + nrm(shape, 0.02)

    n_pages = PAST_LEN // PAGE_SIZE
    n_phys = (DEC_BATCH * n_pages * 5) // 4
    perm = jax.random.permutation(next(ks), n_phys)
    page_table = perm[:DEC_BATCH * n_pages].reshape(DEC_BATCH, n_pages).astype(jnp.int32)
    sd = D_MODEL ** -0.5
    return {
        'x_prompt': nrm((BATCH, SEQ, D_MODEL), 1.0),
        'x_sample': nrm((DEC_BATCH, DEC_SEQ, D_MODEL), 1.0),
        'cache_fox_k': nrm((DEPTH, n_phys, PAGE_SIZE, H_C, HD_C), 1.0),
        'cache_fox_v': nrm((DEPTH, n_phys, PAGE_SIZE, H_C, HD_C), 1.0),
        'cache_fox_lf': jax.nn.log_sigmoid(F_BIAS + nrm((DEPTH, n_phys, PAGE_SIZE, H_C), 1.0)),
        'cache_mem_k': nrm((DEPTH, DEC_BATCH, N_MEM, H_M, HD_M), 1.0),
        'cache_mem_v': nrm((DEPTH, DEC_BATCH, N_MEM, H_M, HD_M), 1.0),
        'state_conv_a': nrm((DEPTH, DEC_BATCH, K_A - 1, D_A), 1.0),
        'state_conv_b': nrm((DEPTH, DEC_BATCH, K_B - 1, D_B), 0.5),
        'page_table': page_table,
        'mem_prompt': nrm((BATCH, N_MEM, D_MODEL), 1.0),
        'n_ffn1': gain((DEPTH, D_MODEL)),
        'w_ffn1_gate': nrm((DEPTH, D_MODEL, D_FF), sd),
        'w_ffn1_up': nrm((DEPTH, D_MODEL, D_FF), sd),
        'w_ffn1_down': nrm((DEPTH, D_FF, D_MODEL), D_FF ** -0.5),
        'n_mix': gain((DEPTH, D_MODEL)),
        'w_in': nrm((DEPTH, D_MODEL, N_IN), sd),
        'b_forget': F_BIAS + nrm((DEPTH, H_C), 0.1),
        'conv_a': nrm((DEPTH, K_A, D_A), K_A ** -0.5),
        'w_a_out': nrm((DEPTH, D_A, D_MODEL), D_A ** -0.5),
        'conv_b': nrm((DEPTH, K_B, D_B), K_B ** -0.5),
        'ln_b_g': gain((DEPTH, D_B)),
        'ln_b_b': nrm((DEPTH, D_B), 0.02),
        'w_b_out': nrm((DEPTH, D_B, D_MODEL), D_B ** -0.5),
        'qn_c': gain((DEPTH, HD_C)),
        'kn_c': gain((DEPTH, HD_C)),
        'w_c_out': nrm((DEPTH, D_C, D_MODEL), D_C ** -0.5),
        'n_mem': gain((DEPTH, D_MODEL)),
        'w_mem_k': nrm((DEPTH, D_MODEL, D_M), sd),
        'w_mem_v': nrm((DEPTH, D_MODEL, D_M), sd),
        'qn_m': gain((DEPTH, HD_M)),
        'kn_m': gain((DEPTH, HD_M)),
        'w_m_out': nrm((DEPTH, D_M, D_MODEL), D_M ** -0.5),
        'w_o': nrm((DEPTH, D_MODEL, D_MODEL), sd),
        'n_ffn2': gain((DEPTH, D_MODEL)),
        'w_ffn2_gate': nrm((DEPTH, D_MODEL, D_FF), sd),
        'w_ffn2_up': nrm((DEPTH, D_MODEL, D_FF), sd),
        'w_ffn2_down': nrm((DEPTH, D_FF, D_MODEL), D_FF ** -0.5),
    }


def reference(x_prompt, x_sample, cache_fox_k, cache_fox_v, cache_fox_lf, cache_mem_k, cache_mem_v,
              state_conv_a, state_conv_b, page_table, mem_prompt,
              n_ffn1, w_ffn1_gate, w_ffn1_up, w_ffn1_down, n_mix, w_in, b_forget,
              conv_a, w_a_out, conv_b, ln_b_g, ln_b_b, w_b_out, qn_c, kn_c, w_c_out,
              n_mem, w_mem_k, w_mem_v, qn_m, kn_m, w_m_out, w_o,
              n_ffn2, w_ffn2_gate, w_ffn2_up, w_ffn2_down):
    n_pages = PAST_LEN // PAGE_SIZE
    db = x_sample.shape[0]
    bp = x_prompt.shape[0]
    xp, xs = x_prompt, x_sample
    fk_p, fv_p, flf_p, fk_s, fv_s, flf_s = [], [], [], [], [], []
    mk_list, mv_list, ca_p, ca_s, cb_p, cb_s = [], [], [], [], [], []
    for l in range(DEPTH):
        p = dict(n_ffn1=n_ffn1[l], w1g=w_ffn1_gate[l], w1u=w_ffn1_up[l], w1d=w_ffn1_down[l],
                 n_mix=n_mix[l], w_in=w_in[l], b_f=b_forget[l],
                 conv_a=conv_a[l], w_a_out=w_a_out[l],
                 conv_b=conv_b[l], ln_b_g=ln_b_g[l], ln_b_b=ln_b_b[l], w_b_out=w_b_out[l],
                 qn_c=qn_c[l], kn_c=kn_c[l], w_c_out=w_c_out[l],
                 qn_m=qn_m[l], w_m_out=w_m_out[l], w_o=w_o[l],
                 n_ffn2=n_ffn2[l], w2g=w_ffn2_gate[l], w2u=w_ffn2_up[l], w2d=w_ffn2_down[l])
        mk_p, mv_p = _mem_kv(mem_prompt, n_mem[l], w_mem_k[l], w_mem_v[l], kn_m[l])
        hist_a = jnp.zeros((bp, K_A - 1, D_A), xp.dtype)
        hist_b = jnp.zeros((bp, K_B - 1, D_B), xp.dtype)
        xp, na, nb_, k_n, v_n, lf_n = _layer(xp, p, hist_a, hist_b, _fox_prompt, mk_p, mv_p)
        fk_p.append(k_n); fv_p.append(v_n); flf_p.append(lf_n)
        mk_list.append(mk_p); mv_list.append(mv_p); ca_p.append(na); cb_p.append(nb_)
        k_past = cache_fox_k[l][page_table].reshape(db, n_pages * PAGE_SIZE, H_C, HD_C)
        v_past = cache_fox_v[l][page_table].reshape(db, n_pages * PAGE_SIZE, H_C, HD_C)
        lf_past = cache_fox_lf[l][page_table].reshape(db, n_pages * PAGE_SIZE, H_C)
        fox_fn = functools.partial(_fox_sample, k_past=k_past, v_past=v_past, lf_past=lf_past)
        xs, na, nb_, k_n, v_n, lf_n = _layer(xs, p, state_conv_a[l], state_conv_b[l], fox_fn,
                                            cache_mem_k[l], cache_mem_v[l])
        fk_s.append(k_n); fv_s.append(v_n); flf_s.append(lf_n)
        ca_s.append(na); cb_s.append(nb_)
    return (xp, xs,
            jnp.stack(fk_p), jnp.stack(fv_p), jnp.stack(flf_p),
            jnp.stack(fk_s), jnp.stack(fv_s), jnp.stack(flf_s),
            jnp.stack(mk_list), jnp.stack(mv_list),
            jnp.stack(ca_p), jnp.stack(ca_s),
            jnp.stack(cb_p), jnp.stack(cb_s))
```

```python
import functools

import jax
import jax.numpy as jnp
from jax import lax
from jax.experimental import pallas as pl
from jax.experimental.pallas import tpu as pltpu

F32 = jnp.float32
BF16 = jnp.bfloat16

D_MODEL = 2048
D_BR = D_MODEL // 2
K_A = 3
K_B = 31
HD_C = 128
H_C = D_BR // HD_C
H_M = 4
HD_M = D_BR // H_M
N_MEM = 256
N_BRANCH = 4
PAGE = 128
NORM_EPS = 1e-6
LN_EPS = 1e-5

LANES = 128
SUBLANES = 8
VMEM_LIMIT = 56 * 1024 * 1024

TM = 640
TF = 512
TN = 512
TN_MERGE = 256
TC = 512
HALO = 32
TQ = 1024
TK = 1024
TQ_M = 512
CS = 256
PAGES_PER_STEP = 8
NEG = -1e30


def _params(*sem):
    return pltpu.CompilerParams(dimension_semantics=sem, vmem_limit_bytes=VMEM_LIMIT)


def _rms_rows(x, g):
    ms = jnp.mean(x * x, axis=-1, keepdims=True)
    return x * lax.rsqrt(ms + NORM_EPS) * g


def _log_sigmoid(x):
    return jnp.minimum(x, 0.0) - jnp.log1p(jnp.exp(-jnp.abs(x)))


def _dot(a, b):
    return jnp.dot(a, b, preferred_element_type=F32)


def _dot_nt(a, b):
    return lax.dot_general(a, b, (((1,), (1,)), ((), ())), preferred_element_type=F32)


def _ffn_body(emit_h, nf, x_ref, n1_ref, wg_ref, wu_ref, wd_ref, *rest):
    if emit_h:
        n2_ref, o_ref, h2_ref, h_sc = rest
    else:
        o_ref, h_sc = rest
    f = pl.program_id(1)

    @pl.when(f == 0)
    def _():
        h_sc[...] = _rms_rows(x_ref[...], n1_ref[...]).astype(BF16)

    h = h_sc[...]
    g = _dot(h, wg_ref[...])
    u = _dot(h, wu_ref[...])
    a = (g * jax.nn.sigmoid(g) * u).astype(BF16)
    d = 0.5 * _dot(a, wd_ref[...])

    @pl.when(f == 0)
    def _():
        o_ref[...] = x_ref[...] + d

    @pl.when(f > 0)
    def _():
        o_ref[...] += d

    if emit_h:
        @pl.when(f == nf - 1)
        def _():
            h2_ref[...] = _rms_rows(o_ref[...], n2_ref[...]).astype(BF16)


def _ffn(x, n1, wg, wu, wd, n2=None):
    rows, d = x.shape
    dff = wg.shape[1]
    nf = dff // TF
    emit_h = n2 is not None
    row = lambda i, f: (i, 0)
    fixed = lambda i, f: (0, 0)
    in_specs = [pl.BlockSpec((TM, d), row), pl.BlockSpec((1, d), fixed),
                pl.BlockSpec((d, TF), lambda i, f: (0, f)), pl.BlockSpec((d, TF), lambda i, f: (0, f)),
                pl.BlockSpec((TF, d), lambda i, f: (f, 0))]
    args = [x, n1.reshape(1, d), wg, wu, wd]
    out_shape = [jax.ShapeDtypeStruct((rows, d), F32)]
    out_specs = [pl.BlockSpec((TM, d), row)]
    if emit_h:
        in_specs.append(pl.BlockSpec((1, d), fixed))
        args.append(n2.reshape(1, d))
        out_shape.append(jax.ShapeDtypeStruct((rows, d), BF16))
        out_specs.append(pl.BlockSpec((TM, d), row))
    res = pl.pallas_call(
        functools.partial(_ffn_body, emit_h, nf),
        grid=(rows // TM, nf), in_specs=in_specs, out_specs=out_specs, out_shape=out_shape,
        scratch_shapes=[pltpu.VMEM((TM, d), BF16)],
        compiler_params=_params("parallel", "arbitrary"), name="ffn_h" if emit_h else "ffn",
    )(*args)
    return res if emit_h else res[0]


def _proj_body(n_w, n_extra, epilogue, h_ref, *refs):
    w_refs = refs[:n_w]
    extra = refs[n_w:n_w + n_extra]
    outs = refs[n_w + n_extra:]
    h = h_ref[...]
    epilogue([_dot(h, w[...]) for w in w_refs], extra, outs)


def _proj(h, w, col_offsets, width, extras, epilogue, out_dtypes, tn=TN, tm=TM, name="proj"):
    rows, d = h.shape
    w_specs = [pl.BlockSpec((d, tn), functools.partial(lambda off, i, j: (0, off + j), off // tn))
               for off in col_offsets]
    e_specs = [pl.BlockSpec(e.shape, lambda i, j: (0, 0)) for e in extras]
    return pl.pallas_call(
        functools.partial(_proj_body, len(col_offsets), len(extras), epilogue),
        grid=(rows // tm, width // tn),
        in_specs=[pl.BlockSpec((tm, d), lambda i, j: (i, 0))] + w_specs + e_specs,
        out_specs=[pl.BlockSpec((tm, tn), lambda i, j: (i, j)) for _ in out_dtypes],
        out_shape=[jax.ShapeDtypeStruct((rows, width), dt) for dt in out_dtypes],
        compiler_params=_params("parallel", "arbitrary"), name=name,
    )(h, *([w] * len(col_offsets)), *extras)


def _head_rms(x, g, hd):
    parts = []
    for c in range(x.shape[1] // hd):
        parts.append(_rms_rows(x[:, c * hd:(c + 1) * hd], g))
    return parts


def _epi_conv_a(accs, extra, outs):
    outs[0][...] = accs[0]
    outs[1][...] = accs[1] * accs[2]


def _epi_glu(accs, extra, outs):
    outs[0][...] = accs[0] * jax.nn.sigmoid(accs[1])


def _epi_fox(accs, extra, outs):
    qn_ref, kn_ref = extra
    q_ref, k_ref, kb_ref, v_ref, vb_ref = outs
    for c, part in enumerate(_head_rms(accs[0], qn_ref[...], HD_C)):
        q_ref[:, c * HD_C:(c + 1) * HD_C] = part.astype(BF16)
    for c, part in enumerate(_head_rms(accs[1], kn_ref[...], HD_C)):
        k_ref[:, c * HD_C:(c + 1) * HD_C] = part
        kb_ref[:, c * HD_C:(c + 1) * HD_C] = part.astype(BF16)
    v_ref[...] = accs[2]
    vb_ref[...] = accs[2].astype(BF16)


def _epi_forget(accs, extra, outs):
    outs[0][...] = _log_sigmoid(accs[0] + extra[0][...])


def _epi_qmem(accs, extra, outs):
    for c, part in enumerate(_head_rms(accs[0], extra[0][...], HD_M)):
        outs[0][:, c * HD_M:(c + 1) * HD_M] = part.astype(BF16)


def _epi_gates(accs, extra, outs):
    outs[0][...] = jax.nn.sigmoid(accs[0])


def _epi_memkv(accs, extra, outs):
    for c, part in enumerate(_head_rms(accs[0], extra[0][...], HD_M)):
        outs[0][:, c * HD_M:(c + 1) * HD_M] = part
    outs[1][...] = accs[1]


def _rms_cast_body(x_ref, g_ref, o_ref):
    o_ref[...] = _rms_rows(x_ref[...], g_ref[...]).astype(BF16)


def _rms_cast(x, g, tm):
    rows, d = x.shape
    return pl.pallas_call(
        _rms_cast_body, grid=(rows // tm,),
        in_specs=[pl.BlockSpec((tm, d), lambda i: (i, 0)), pl.BlockSpec((1, d), lambda i: (0, 0))],
        out_specs=pl.BlockSpec((tm, d), lambda i: (i, 0)),
        out_shape=jax.ShapeDtypeStruct((rows, d), BF16),
        compiler_params=_params("parallel"), name="rms_cast",
    )(x, g.reshape(1, d))


def _ln_silu(x, g, b):
    mu = jnp.mean(x, axis=-1, keepdims=True)
    xc = x - mu
    var = jnp.mean(xc * xc, axis=-1, keepdims=True)
    z = xc * lax.rsqrt(var + LN_EPS) * g + b
    return z * jax.nn.sigmoid(z)


def _conv_prompt_body(tiles_per_seq, ba_ref, cx_ref, cxh_ref, gl_ref, glh_ref, wa_ref, wb_ref,
                      lg_ref, lb_ref, za_ref, zb_ref, sa, sb):
    first = (pl.program_id(0) % tiles_per_seq) == 0
    sa[0:HALO, :] = jnp.where(first, 0.0, cxh_ref[...])
    sa[HALO:, :] = cx_ref[...]
    sb[0:HALO, :] = jnp.where(first, 0.0, glh_ref[...])
    sb[HALO:, :] = gl_ref[...]
    acc = None
    for k in range(K_A):
        term = wa_ref[k:k + 1, :] * sa[pl.ds(HALO - (K_A - 1) + k, TC), :]
        acc = term if acc is None else acc + term
    za_ref[...] = (ba_ref[...] * acc).astype(BF16)
    acc = None
    for k in range(K_B):
        term = wb_ref[k:k + 1, :] * sb[pl.ds(HALO - (K_B - 1) + k, TC), :]
        acc = term if acc is None else acc + term
    zb_ref[...] = _ln_silu(acc, lg_ref[...], lb_ref[...]).astype(BF16)


def _conv_prompt(b_a, cx, glu, n_rows, seq, wa, wb, ln_g, ln_b):
    c = cx.shape[1]
    tile = lambda i: (i, 0)
    halo = lambda i: (jnp.maximum(i * (TC // HALO) - 1, 0), 0)
    fixed = lambda i: (0, 0)
    return pl.pallas_call(
        functools.partial(_conv_prompt_body, seq // TC),
        grid=(n_rows // TC,),
        in_specs=[pl.BlockSpec((TC, c), tile), pl.BlockSpec((TC, c), tile), pl.BlockSpec((HALO, c), halo),
                  pl.BlockSpec((TC, c), tile), pl.BlockSpec((HALO, c), halo),
                  pl.BlockSpec((K_A, c), fixed), pl.BlockSpec((K_B, c), fixed),
                  pl.BlockSpec((1, c), fixed), pl.BlockSpec((1, c), fixed)],
        out_specs=[pl.BlockSpec((TC, c), tile), pl.BlockSpec((TC, c), tile)],
        out_shape=[jax.ShapeDtypeStruct((n_rows, c), BF16)] * 2,
        scratch_shapes=[pltpu.VMEM((TC + HALO, c), F32)] * 2,
        compiler_params=_params("parallel"), name="conv_prompt",
    )(b_a, cx, cx, glu, glu, wa, wb, ln_g.reshape(1, c), ln_b.reshape(1, c))


def _conv_sample_body(n_t, ba_ref, ea_ref, eb_ref, wa_ref, wb_ref, lg_ref, lb_ref, za_ref, zb_ref):
    for t in range(n_t):
        acc = None
        for k in range(K_A):
            term = wa_ref[k:k + 1, :] * ea_ref[t + k]
            acc = term if acc is None else acc + term
        za_ref[t] = (ba_ref[t] * acc).astype(BF16)
        acc = None
        for k in range(K_B):
            term = wb_ref[k:k + 1, :] * eb_ref[t + k]
            acc = term if acc is None else acc + term
        zb_ref[t] = _ln_silu(acc, lg_ref[...], lb_ref[...]).astype(BF16)


def _conv_sample(ba_t, ext_a_t, ext_b_t, wa, wb, ln_g, ln_b):
    n_t, nb, c = ba_t.shape
    return pl.pallas_call(
        functools.partial(_conv_sample_body, n_t),
        out_shape=[jax.ShapeDtypeStruct((n_t, nb, c), BF16)] * 2,
        compiler_params=pltpu.CompilerParams(vmem_limit_bytes=VMEM_LIMIT), name="conv_sample",
    )(ba_t, ext_a_t, ext_b_t, wa, wb, ln_g.reshape(1, c), ln_b.reshape(1, c))


def _cumsum_body(x_ref, o_ref, carry):
    @pl.when(pl.program_id(1) == 0)
    def _():
        carry[...] = jnp.zeros_like(carry)

    r = lax.broadcasted_iota(jnp.int32, (CS, CS), 0)
    c = lax.broadcasted_iota(jnp.int32, (CS, CS), 1)
    tri = jnp.where(c <= r, 1.0, 0.0).astype(F32)
    out = jnp.dot(tri, x_ref[...], preferred_element_type=F32, precision=lax.Precision.HIGHEST) + carry[...]
    o_ref[...] = out
    carry[...] = out[CS - 1:CS, :]


def _cumsum_rows(x):
    nb, rows, c = x.shape
    return pl.pallas_call(
        _cumsum_body, grid=(nb, rows // CS),
        in_specs=[pl.BlockSpec((None, CS, c), lambda b, i: (b, i, 0))],
        out_specs=pl.BlockSpec((None, CS, c), lambda b, i: (b, i, 0)),
        out_shape=jax.ShapeDtypeStruct((nb, rows, c), F32),
        scratch_shapes=[pltpu.VMEM((1, c), F32)],
        compiler_params=_params("parallel", "arbitrary"), name="cumsum",
    )(x)


def _fox_prompt_body(nk, q_ref, k_ref, v_ref, cq_ref, ck_ref, o_ref, m_sc, l_sc, acc_sc, cq_sc):
    h = pl.program_id(1)
    i = pl.program_id(2)
    j = pl.program_id(3)

    @pl.when(j == 0)
    def _():
        m_sc[...] = jnp.full_like(m_sc, NEG)
        l_sc[...] = jnp.zeros_like(l_sc)
        acc_sc[...] = jnp.zeros_like(acc_sc)
        lane = lax.broadcasted_iota(jnp.int32, cq_ref.shape, 1)
        cq_sc[...] = jnp.sum(jnp.where(lane == h, cq_ref[...], 0.0), axis=1, keepdims=True)

    @pl.when(j <= i)
    def _():
        s = _dot_nt(q_ref[...], k_ref[...]) * (HD_C ** -0.5)
        s = s + (cq_sc[...] - ck_ref[...])
        row = i * TQ + lax.broadcasted_iota(jnp.int32, (TQ, TK), 0)
        col = j * TK + lax.broadcasted_iota(jnp.int32, (TQ, TK), 1)
        s = jnp.where(col <= row, s, -jnp.inf)
        m_new = jnp.maximum(m_sc[...], jnp.max(s, axis=1, keepdims=True))
        alpha = jnp.exp(m_sc[...] - m_new)
        p = jnp.exp(s - m_new)
        l_sc[...] = alpha * l_sc[...] + jnp.sum(p, axis=1, keepdims=True)
        acc_sc[...] = alpha * acc_sc[...] + _dot(p.astype(BF16), v_ref[...])
        m_sc[...] = m_new

    @pl.when(j == nk - 1)
    def _():
        o_ref[...] = (acc_sc[...] / l_sc[...]).astype(BF16)


def _fox_prompt(qb, kb, vb, c_tok, c_row, nb, seq):
    nq, nk = seq // TQ, seq // TK
    return pl.pallas_call(
        functools.partial(_fox_prompt_body, nk),
        grid=(nb, H_C, nq, nk),
        in_specs=[pl.BlockSpec((TQ, HD_C), lambda b, h, i, j: (b * nq + i, h)),
                  pl.BlockSpec((TK, HD_C), lambda b, h, i, j: (b * nk + jnp.minimum(j, i), h)),
                  pl.BlockSpec((TK, HD_C), lambda b, h, i, j: (b * nk + jnp.minimum(j, i), h)),
                  pl.BlockSpec((TQ, LANES), lambda b, h, i, j: (b * nq + i, 0)),
                  pl.BlockSpec((None, None, 1, TK), lambda b, h, i, j: (b, h, 0, jnp.minimum(j, i)))],
        out_specs=pl.BlockSpec((TQ, HD_C), lambda b, h, i, j: (b * nq + i, h)),
        out_shape=jax.ShapeDtypeStruct((nb * seq, H_C * HD_C), BF16),
        scratch_shapes=[pltpu.VMEM((TQ, 1), F32), pltpu.VMEM((TQ, 1), F32), pltpu.VMEM((TQ, HD_C), F32),
                        pltpu.VMEM((TQ, 1), F32)],
        compiler_params=_params("parallel", "parallel", "parallel", "arbitrary"), name="fox_prompt",
    )(qb, kb, vb, c_tok, c_row)


def _fox_sample_body(n_steps, pt_ref, q_ref, kn_ref, vn_ref, lfr_ref, lfc_ref, *rest):
    g_pages = PAGES_PER_STEP
    k_refs = rest[:g_pages]
    v_refs = rest[g_pages:2 * g_pages]
    lf_refs = rest[2 * g_pages:3 * g_pages]
    o_ref, m_sc, l_sc, acc_sc, cq_sc, run_sc, lf_sc = rest[3 * g_pages:]
    step = pl.program_id(1)
    n_q = q_ref.shape[0]
    n_key = PAGE * H_C
    scale = HD_C ** -0.5
    q = q_ref[...]

    @pl.when(step == 0)
    def _():
        r = lax.broadcasted_iota(jnp.int32, (n_q, n_q), 0)
        c = lax.broadcasted_iota(jnp.int32, (n_q, n_q), 1)
        ok = ((r % H_C) == (c % H_C)) & (c // H_C <= r // H_C)
        ok_t = ((r % H_C) == (c % H_C)) & (r // H_C <= c // H_C)
        cq_col = jnp.sum(jnp.where(ok, lfr_ref[...], 0.0), axis=1, keepdims=True)
        cq_row = jnp.sum(jnp.where(ok_t, lfc_ref[...], 0.0), axis=0, keepdims=True)
        s = _dot_nt(q, kn_ref[...]) * scale + (cq_col - cq_row)
        s = jnp.where(ok, s, -jnp.inf)
        m = jnp.max(s, axis=1, keepdims=True)
        p = jnp.exp(s - m)
        m_sc[...] = m
        l_sc[...] = jnp.sum(p, axis=1, keepdims=True)
        acc_sc[...] = _dot(p.astype(BF16), vn_ref[...])
        cq_sc[...] = cq_col
        run_sc[...] = jnp.zeros_like(run_sc)

    for g in range(g_pages):
        lf_sc[g:g + 1, :] = lf_refs[g][...]
    lf = lf_sc[...]
    lane = lax.broadcasted_iota(jnp.int32, lf.shape, 1)
    incl = lf
    sh = H_C
    while sh < n_key:
        incl = incl + jnp.where(lane < n_key - sh, pltpu.roll(incl, n_key - sh, axis=1), 0.0)
        sh *= 2
    tot = jnp.where(lane < H_C, incl, 0.0)
    sh = H_C
    while sh < n_key:
        tot = tot + pltpu.roll(tot, sh, axis=1)
        sh *= 2
    after = incl - lf

    r = lax.broadcasted_iota(jnp.int32, (n_q, n_key), 0)
    c = lax.broadcasted_iota(jnp.int32, (n_q, n_key), 1)
    same_head = (r % H_C) == (c % H_C)
    m = m_sc[...]
    l = l_sc[...]
    acc = acc_sc[...]
    run = run_sc[...]
    cq = cq_sc[...]
    for g in range(g_pages):
        k2 = k_refs[g][...].reshape(n_key, HD_C).astype(BF16)
        v2 = v_refs[g][...].reshape(n_key, HD_C).astype(BF16)
        s = _dot_nt(q, k2) * scale + (cq + (after[g:g + 1, :] + run))
        s = jnp.where(same_head, s, -jnp.inf)
        m_new = jnp.maximum(m, jnp.max(s, axis=1, keepdims=True))
        alpha = jnp.exp(m - m_new)
        p = jnp.exp(s - m_new)
        l = alpha * l + jnp.sum(p, axis=1, keepdims=True)
        acc = alpha * acc + _dot(p.astype(BF16), v2)
        m = m_new
        run = run + tot[g:g + 1, :]
    m_sc[...] = m
    l_sc[...] = l
    acc_sc[...] = acc
    run_sc[...] = run

    @pl.when(step == n_steps - 1)
    def _():
        o_ref[...] = (acc / l).astype(BF16)


def _fox_sample(layer, page_table, q, k_new, v_new, lf_row, lf_col, cache_k, cache_v, cache_lf):
    nb, n_q, _ = q.shape
    n_pages = page_table.shape[1]
    g_pages = PAGES_PER_STEP
    n_steps = n_pages // g_pages
    pt = page_table.reshape(-1)

    def page(g, b, s, pt_ref):
        return pt_ref[b * n_pages + (n_pages - 1 - (s * g_pages + g))]

    per_b = lambda b, s, pt_ref: (b, 0, 0)
    kv_specs = [pl.BlockSpec((None, None, PAGE, H_C, HD_C),
                             functools.partial(lambda g, b, s, pt_ref: (layer, page(g, b, s, pt_ref), 0, 0, 0), g))
                for g in range(g_pages)]
    lf_specs = [pl.BlockSpec((None, None, 1, PAGE * H_C),
                             functools.partial(lambda g, b, s, pt_ref: (layer, page(g, b, s, pt_ref), 0, 0), g))
                for g in range(g_pages)]
    grid_spec = pltpu.PrefetchScalarGridSpec(
        num_scalar_prefetch=1, grid=(nb, n_steps),
        in_specs=[pl.BlockSpec((None, n_q, HD_C), per_b), pl.BlockSpec((None, n_q, HD_C), per_b),
                  pl.BlockSpec((None, n_q, HD_C), per_b), pl.BlockSpec((None, 1, n_q), per_b),
                  pl.BlockSpec((None, n_q, 1), per_b)] + kv_specs + kv_specs + lf_specs,
        out_specs=pl.BlockSpec((None, n_q, HD_C), per_b),
        scratch_shapes=[pltpu.VMEM((n_q, 1), F32), pltpu.VMEM((n_q, 1), F32), pltpu.VMEM((n_q, HD_C), F32),
                        pltpu.VMEM((n_q, 1), F32), pltpu.VMEM((1, PAGE * H_C), F32),
                        pltpu.VMEM((g_pages, PAGE * H_C), F32)])
    return pl.pallas_call(
        functools.partial(_fox_sample_body, n_steps), grid_spec=grid_spec,
        out_shape=jax.ShapeDtypeStruct((nb, n_q, HD_C), BF16),
        compiler_params=_params("parallel", "arbitrary"), name="fox_sample",
    )(pt, q, k_new, v_new, lf_row, lf_col, *([cache_k] * g_pages), *([cache_v] * g_pages),
      *([cache_lf] * g_pages))


def _mem_attn_body(q_ref, mk_ref, mv_ref, o_ref):
    q = q_ref[...].astype(BF16)
    for h in range(H_M):
        cols = slice(h * HD_M, (h + 1) * HD_M)
        s = _dot_nt(q[:, cols], mk_ref[:, cols].astype(BF16)) * (HD_M ** -0.5)
        m = jnp.max(s, axis=1, keepdims=True)
        p = jnp.exp(s - m)
        l = jnp.sum(p, axis=1, keepdims=True)
        o = _dot(p.astype(BF16), mv_ref[:, cols].astype(BF16)) / l
        o_ref[:, cols] = o.astype(o_ref.dtype)


def _mem_attn(q, mk, mv, tq, out_dtype):
    nb, rows, d = q.shape
    return pl.pallas_call(
        _mem_attn_body, grid=(nb, rows // tq),
        in_specs=[pl.BlockSpec((None, tq, d), lambda b, i: (b, i, 0)),
                  pl.BlockSpec((None, N_MEM, d), lambda b, i: (b, 0, 0)),
                  pl.BlockSpec((None, N_MEM, d), lambda b, i: (b, 0, 0))],
        out_specs=pl.BlockSpec((None, tq, d), lambda b, i: (b, i, 0)),
        out_shape=jax.ShapeDtypeStruct((nb, rows, d), out_dtype),
        compiler_params=_params("parallel", "arbitrary"), name="mem_attn",
    )(q, mk, mv)


def _merge_body(x_ref, *refs):
    z_refs = refs[:N_BRANCH]
    g_refs = refs[N_BRANCH:2 * N_BRANCH]
    w_refs = refs[2 * N_BRANCH:3 * N_BRANCH]
    wo_ref, o_ref = refs[3 * N_BRANCH:]
    n = pl.program_id(1)
    merged = None
    for z, g, w in zip(z_refs, g_refs, w_refs):
        term = g[...] * _dot(z[...], w[...])
        merged = term if merged is None else merged + term
    d = _dot(merged.astype(BF16), wo_ref[...])

    @pl.when(n == 0)
    def _():
        o_ref[...] = x_ref[...] + d

    @pl.when(n > 0)
    def _():
        o_ref[...] += d


def _merge(x, zs, gates, w_outs, w_o):
    rows, d = x.shape
    tn = TN_MERGE
    nn = d // tn
    row = lambda i, n: (i, 0)
    in_specs = [pl.BlockSpec((TM, d), row)]
    in_specs += [pl.BlockSpec((TM, D_BR), row) for _ in range(N_BRANCH)]
    in_specs += [pl.BlockSpec((TM, tn), functools.partial(lambda br, i, n: (i, br * nn + n), br))
                 for br in range(N_BRANCH)]
    in_specs += [pl.BlockSpec((D_BR, tn), lambda i, n: (0, n)) for _ in range(N_BRANCH)]
    in_specs += [pl.BlockSpec((tn, d), lambda i, n: (n, 0))]
    return pl.pallas_call(
        _merge_body, grid=(rows // TM, nn), in_specs=in_specs,
        out_specs=pl.BlockSpec((TM, d), row),
        out_shape=jax.ShapeDtypeStruct((rows, d), F32),
        compiler_params=_params("parallel", "arbitrary"), name="merge",
    )(x, *zs, *([gates] * N_BRANCH), *w_outs, w_o)


def kernel(x_prompt, x_sample, cache_fox_k, cache_fox_v, cache_fox_lf, cache_mem_k, cache_mem_v, state_conv_a, state_conv_b, page_table, mem_prompt, n_ffn1, w_ffn1_gate, w_ffn1_up, w_ffn1_down, n_mix, w_in, b_forget, conv_a, w_a_out, conv_b, ln_b_g, ln_b_b, w_b_out, qn_c, kn_c, w_c_out, n_mem, w_mem_k, w_mem_v, qn_m, kn_m, w_m_out, w_o, n_ffn2, w_ffn2_gate, w_ffn2_up, w_ffn2_down):
    depth = w_in.shape[0]
    bp, seq, d = x_prompt.shape
    db, dseq, _ = x_sample.shape
    n_p = bp * seq
    n_s = db * dseq
    n_phys = cache_fox_lf.shape[1]

    x = jnp.concatenate([x_prompt.reshape(n_p, d), x_sample.reshape(n_s, d)], axis=0)
    mem_rows = mem_prompt.reshape(bp * N_MEM, d)
    cache_lf_rows = cache_fox_lf.reshape(depth, n_phys, 1, PAGE * H_C)
    cache_mk = cache_mem_k.reshape(depth, db, N_MEM, D_BR)
    cache_mv = cache_mem_v.reshape(depth, db, N_MEM, D_BR)

    fox_end = 8 * D_BR
    off_ba, off_ca, off_xa, off_g1, off_g2, off_q, off_k, off_v = [s * D_BR for s in range(8)]
    off_qm = fox_end
    off_gates = fox_end + D_BR
    off_f = off_gates + N_BRANCH * D_MODEL

    outs = [[] for _ in range(12)]
    for l in range(depth):
        wg1, wu1, wd1 = w_ffn1_gate[l].astype(BF16), w_ffn1_up[l].astype(BF16), w_ffn1_down[l].astype(BF16)
        wg2, wu2, wd2 = w_ffn2_gate[l].astype(BF16), w_ffn2_up[l].astype(BF16), w_ffn2_down[l].astype(BF16)
        w_f = jnp.pad(w_in[l][:, fox_end:fox_end + H_C], ((0, 0), (0, LANES - H_C)))
        w_all = jnp.concatenate([w_in[l][:, :fox_end], w_in[l][:, fox_end + H_C:], w_f], axis=1).astype(BF16)
        b_f = jnp.pad(b_forget[l], (0, LANES - H_C)).reshape(1, LANES)
        w_branch = [w_a_out[l].astype(BF16), w_b_out[l].astype(BF16), w_c_out[l].astype(BF16),
                    w_m_out[l].astype(BF16)]
        w_mkv = jnp.concatenate([w_mem_k[l], w_mem_v[l]], axis=1).astype(BF16)

        x, h = _ffn(x, n_ffn1[l], wg1, wu1, wd1, n_mix[l])

        b_a, cx = _proj(h, w_all, [off_ba, off_ca, off_xa], D_BR, [], _epi_conv_a, [F32, F32], name="proj_conv_a")
        (glu,) = _proj(h, w_all, [off_g1, off_g2], D_BR, [], _epi_glu, [F32], name="proj_glu")
        q_c, k_c, kb_c, v_c, vb_c = _proj(
            h, w_all, [off_q, off_k, off_v], D_BR, [qn_c[l].reshape(1, HD_C), kn_c[l].reshape(1, HD_C)],
            _epi_fox, [BF16, F32, BF16, F32, BF16], name="proj_fox")
        (lf_pad,) = _proj(h, w_all, [off_f], LANES, [b_f], _epi_forget, [F32], tn=LANES, name="proj_forget")
        (q_m,) = _proj(h, w_all, [off_qm], D_BR, [qn_m[l].reshape(1, HD_M)], _epi_qmem, [BF16], name="proj_qmem")
        (gates,) = _proj(h, w_all, [off_gates], N_BRANCH * D_MODEL, [], _epi_gates, [F32], name="proj_gates")

        m_h = _rms_cast(mem_rows, n_mem[l], N_MEM)
        mk_p, mv_p = _proj(m_h, w_mkv, [0, D_BR], D_BR, [kn_m[l].reshape(1, HD_M)], _epi_memkv, [F32, F32],
                           tm=N_MEM, name="proj_memkv")

        za_p, zb_p = _conv_prompt(b_a, cx, glu, n_p, seq, conv_a[l], conv_b[l], ln_b_g[l], ln_b_b[l])
        cx_s = cx[n_p:].reshape(db, dseq, D_BR)
        glu_s = glu[n_p:].reshape(db, dseq, D_BR)
        ext_a = jnp.concatenate([state_conv_a[l], cx_s], axis=1)
        ext_b = jnp.concatenate([state_conv_b[l], glu_s], axis=1)
        ba_t = jnp.swapaxes(b_a[n_p:].reshape(db, dseq, D_BR), 0, 1)
        za_s, zb_s = _conv_sample(ba_t, jnp.swapaxes(ext_a, 0, 1), jnp.swapaxes(ext_b, 0, 1),
                                  conv_a[l], conv_b[l], ln_b_g[l], ln_b_b[l])
        z_a = jnp.concatenate([za_p, jnp.swapaxes(za_s, 0, 1).reshape(n_s, D_BR)], axis=0)
        z_b = jnp.concatenate([zb_p, jnp.swapaxes(zb_s, 0, 1).reshape(n_s, D_BR)], axis=0)

        lf = lf_pad[:, :H_C]
        c_tok = _cumsum_rows(lf_pad[:n_p].reshape(bp, seq, LANES))
        c_row = jnp.swapaxes(c_tok[:, :, :H_C], 1, 2).reshape(bp, H_C, 1, seq)
        yc_p = _fox_prompt(q_c, kb_c, vb_c, c_tok.reshape(n_p, LANES), c_row, bp, seq)
        lf_s = lf[n_p:].reshape(db, dseq * H_C)
        yc_s = _fox_sample(
            l, page_table, q_c[n_p:].reshape(db, dseq * H_C, HD_C), kb_c[n_p:].reshape(db, dseq * H_C, HD_C),
            vb_c[n_p:].reshape(db, dseq * H_C, HD_C), lf_s.reshape(db, 1, dseq * H_C),
            lf_s.reshape(db, dseq * H_C, 1), cache_fox_k, cache_fox_v, cache_lf_rows)
        y_c = jnp.concatenate([yc_p, yc_s.reshape(n_s, D_BR)], axis=0)

        ym_p = _mem_attn(q_m[:n_p].reshape(bp, seq, D_BR), mk_p.reshape(bp, N_MEM, D_BR),
                         mv_p.reshape(bp, N_MEM, D_BR), TQ_M, BF16)
        qm_s = jnp.pad(q_m[n_p:].reshape(db, dseq, D_BR).astype(F32), ((0, 0), (0, SUBLANES - dseq), (0, 0)))
        ym_s = _mem_attn(qm_s, cache_mk[l], cache_mv[l], SUBLANES, F32)[:, :dseq]
        y_m = jnp.concatenate([ym_p.reshape(n_p, D_BR), ym_s.reshape(n_s, D_BR).astype(BF16)], axis=0)

        x = _merge(x, [z_a, z_b, y_c, y_m], gates, w_branch, w_o[l].astype(BF16))
        x = _ffn(x, n_ffn2[l], wg2, wu2, wd2)

        per_layer = [
            k_c[:n_p].reshape(bp, seq, H_C, HD_C), v_c[:n_p].reshape(bp, seq, H_C, HD_C),
            lf[:n_p].reshape(bp, seq, H_C),
            k_c[n_p:].reshape(db, dseq, H_C, HD_C), v_c[n_p:].reshape(db, dseq, H_C, HD_C),
            lf[n_p:].reshape(db, dseq, H_C),
            mk_p.reshape(bp, N_MEM, H_M, HD_M), mv_p.reshape(bp, N_MEM, H_M, HD_M),
            cx[:n_p].reshape(bp, seq, D_BR)[:, seq - (K_A - 1):], ext_a[:, dseq:],
            glu[:n_p].reshape(bp, seq, D_BR)[:, seq - (K_B - 1):], ext_b[:, dseq:],
        ]
        for acc, val in zip(outs, per_layer):
            acc.append(val)

    return (x[:n_p].reshape(bp, seq, d), x[n_p:].reshape(db, dseq, d)) + tuple(jnp.stack(o) for o in outs)
```

```python
import functools

import jax
import jax.numpy as jnp
from jax import lax
from jax.experimental import pallas as pl
from jax.experimental.pallas import tpu as pltpu

F32 = jnp.float32
BF16 = jnp.bfloat16

D_MODEL = 2048
D_BR = D_MODEL // 2
K_A = 3
K_B = 31
HD_C = 128
H_C = D_BR // HD_C
H_M = 4
HD_M = D_BR // H_M
N_MEM = 256
N_BRANCH = 4
PAGE = 128
NORM_EPS = 1e-6
LN_EPS = 1e-5

LANES = 128
SUBLANES = 8
VMEM_LIMIT = 56 * 1024 * 1024

TM = 1040
TM_FFN = 1040
TM_MERGE = 640
TF = 256
DOWN_CHUNK = 512
TN = 512
TN_MERGE = 256
TC = 512
HALO = 32
CONV_ROWS = 32
TQ = 1024
TK = 1024
TQ_M = 512
CS = 256
PAGES_PER_STEP = 8
NEG = -1e30


def _params(*sem):
    return pltpu.CompilerParams(dimension_semantics=sem, vmem_limit_bytes=VMEM_LIMIT)


def _rms_rows(x, g):
    ms = jnp.mean(x * x, axis=-1, keepdims=True)
    return x * lax.rsqrt(ms + NORM_EPS) * g


def _log_sigmoid(x):
    return jnp.minimum(x, 0.0) - jnp.log1p(jnp.exp(-jnp.abs(x)))


def _dot(a, b):
    return jnp.dot(a, b, preferred_element_type=F32)


def _dot_nt(a, b):
    return lax.dot_general(a, b, (((1,), (1,)), ((), ())), preferred_element_type=F32)


def _ffn_body(emit_h, nf, x_ref, n1_ref, wg_ref, wu_ref, wd_ref, *rest):
    if emit_h:
        n2_ref, o_ref, h2_ref, h_sc = rest
    else:
        o_ref, h_sc = rest
    f = pl.program_id(1)

    @pl.when(f == 0)
    def _():
        x = x_ref[...]
        h_sc[...] = _rms_rows(x, n1_ref[...]).astype(BF16)
        o_ref[...] = x

    h = h_sc[...]
    g = _dot(h, wg_ref[...].astype(BF16))
    u = _dot(h, wu_ref[...].astype(BF16))
    a = (g * jax.nn.sigmoid(g) * u).astype(BF16)
    for c in range(o_ref.shape[1] // DOWN_CHUNK):
        cols = slice(c * DOWN_CHUNK, (c + 1) * DOWN_CHUNK)
        o_ref[:, cols] += 0.5 * _dot(a, wd_ref[:, cols].astype(BF16))

    if emit_h:
        @pl.when(f == nf - 1)
        def _():
            h2_ref[...] = _rms_rows(o_ref[...], n2_ref[...]).astype(BF16)


def _ffn(layer, x, n1, wg, wu, wd, n2=None):
    rows, d = x.shape
    dff = wg.shape[2]
    nf = dff // TF
    emit_h = n2 is not None
    row = lambda i, f: (i, 0)
    fixed = lambda i, f: (0, 0)
    in_specs = [pl.BlockSpec((TM_FFN, d), row, pipeline_mode=pl.Buffered(1)), pl.BlockSpec((1, d), fixed),
                pl.BlockSpec((None, d, TF), lambda i, f: (layer, 0, f)),
                pl.BlockSpec((None, d, TF), lambda i, f: (layer, 0, f)),
                pl.BlockSpec((None, TF, d), lambda i, f: (layer, f, 0))]
    args = [x, n1.reshape(1, d), wg, wu, wd]
    out_shape = [jax.ShapeDtypeStruct((rows, d), F32)]
    out_specs = [pl.BlockSpec((TM_FFN, d), row)]
    if emit_h:
        in_specs.append(pl.BlockSpec((1, d), fixed))
        args.append(n2.reshape(1, d))
        out_shape.append(jax.ShapeDtypeStruct((rows, d), BF16))
        out_specs.append(pl.BlockSpec((TM_FFN, d), row))
    res = pl.pallas_call(
        functools.partial(_ffn_body, emit_h, nf),
        grid=(rows // TM_FFN, nf), in_specs=in_specs, out_specs=out_specs, out_shape=out_shape,
        scratch_shapes=[pltpu.VMEM((TM_FFN, d), BF16)],
        compiler_params=_params("parallel", "arbitrary"), name="ffn_h" if emit_h else "ffn",
    )(*args)
    return res if emit_h else res[0]


def _proj_body(n_w, n_extra, epilogue, h_ref, *refs):
    w_refs = refs[:n_w]
    extra = refs[n_w:n_w + n_extra]
    outs = refs[n_w + n_extra:]
    h = h_ref[...]
    epilogue([_dot(h, w[...]) for w in w_refs], extra, outs)


def _proj(layer, h, w, col_offsets, width, extras, epilogue, out_dtypes, tn=TN, tm=TM, name="proj"):
    rows, d = h.shape
    w_specs = [pl.BlockSpec((None, d, tn), functools.partial(lambda off, i, j: (layer, 0, off + j), off // tn))
               for off in col_offsets]
    e_specs = [pl.BlockSpec(e.shape, lambda i, j: (0, 0)) for e in extras]
    return pl.pallas_call(
        functools.partial(_proj_body, len(col_offsets), len(extras), epilogue),
        grid=(rows // tm, width // tn),
        in_specs=[pl.BlockSpec((tm, d), lambda i, j: (i, 0))] + w_specs + e_specs,
        out_specs=[pl.BlockSpec((tm, tn), lambda i, j: (i, j)) for _ in out_dtypes],
        out_shape=[jax.ShapeDtypeStruct((rows, width), dt) for dt in out_dtypes],
        compiler_params=_params("parallel", "arbitrary"), name=name,
    )(h, *([w] * len(col_offsets)), *extras)


def _head_rms(x, g, hd):
    parts = []
    for c in range(x.shape[1] // hd):
        parts.append(_rms_rows(x[:, c * hd:(c + 1) * hd], g))
    return parts


def _epi_conv_a(accs, extra, outs):
    outs[0][...] = accs[0]
    outs[1][...] = accs[1] * accs[2]


def _epi_glu(accs, extra, outs):
    outs[0][...] = accs[0] * jax.nn.sigmoid(accs[1])


def _epi_fox(accs, extra, outs):
    qn_ref, kn_ref = extra
    q_ref, k_ref, kb_ref, v_ref, vb_ref = outs
    for c, part in enumerate(_head_rms(accs[0], qn_ref[...], HD_C)):
        q_ref[:, c * HD_C:(c + 1) * HD_C] = part.astype(BF16)
    for c, part in enumerate(_head_rms(accs[1], kn_ref[...], HD_C)):
        k_ref[:, c * HD_C:(c + 1) * HD_C] = part
        kb_ref[:, c * HD_C:(c + 1) * HD_C] = part.astype(BF16)
    v_ref[...] = accs[2]
    vb_ref[...] = accs[2].astype(BF16)


def _epi_forget(accs, extra, outs):
    outs[0][...] = _log_sigmoid(accs[0] + extra[0][...])


def _epi_qmem(accs, extra, outs):
    for c, part in enumerate(_head_rms(accs[0], extra[0][...], HD_M)):
        outs[0][:, c * HD_M:(c + 1) * HD_M] = part.astype(BF16)


def _epi_memkv(accs, extra, outs):
    for c, part in enumerate(_head_rms(accs[0], extra[0][...], HD_M)):
        outs[0][:, c * HD_M:(c + 1) * HD_M] = part
    outs[1][...] = accs[1]


def _rms_cast_body(x_ref, g_ref, o_ref):
    o_ref[...] = _rms_rows(x_ref[...], g_ref[...]).astype(BF16)


def _rms_cast(x, g, tm):
    rows, d = x.shape
    return pl.pallas_call(
        _rms_cast_body, grid=(rows // tm,),
        in_specs=[pl.BlockSpec((tm, d), lambda i: (i, 0)), pl.BlockSpec((1, d), lambda i: (0, 0))],
        out_specs=pl.BlockSpec((tm, d), lambda i: (i, 0)),
        out_shape=jax.ShapeDtypeStruct((rows, d), BF16),
        compiler_params=_params("parallel"), name="rms_cast",
    )(x, g.reshape(1, d))


def _ln_silu(x, g, b):
    mu = jnp.mean(x, axis=-1, keepdims=True)
    xc = x - mu
    var = jnp.mean(xc * xc, axis=-1, keepdims=True)
    z = xc * lax.rsqrt(var + LN_EPS) * g + b
    return z * jax.nn.sigmoid(z)


def _conv_prompt_body(tiles_per_seq, ba_ref, cx_ref, cxh_ref, gl_ref, glh_ref, wa_ref, wb_ref,
                      lg_ref, lb_ref, za_ref, zb_ref, sa, sb):
    first = (pl.program_id(0) % tiles_per_seq) == 0
    sa[0:HALO, :] = jnp.where(first, 0.0, cxh_ref[...])
    sa[HALO:, :] = cx_ref[...]
    sb[0, 0:HALO, :] = jnp.where(first, 0.0, glh_ref[...])
    sb[0, HALO:, :] = gl_ref[...]
    acc = None
    for k in range(K_A):
        term = wa_ref[k:k + 1, :] * sa[pl.ds(HALO - (K_A - 1) + k, TC), :]
        acc = term if acc is None else acc + term
    za_ref[...] = (ba_ref[...] * acc).astype(BF16)

    n_shift = TC + HALO - SUBLANES
    for s in range(1, SUBLANES):
        sb[s, 0:n_shift, :] = sb[0, pl.ds(s, n_shift), :]

    def chunk(ci, carry):
        base = pl.multiple_of(ci * CONV_ROWS, CONV_ROWS)
        acc = None
        for k in range(K_B):
            whole, s = divmod(HALO - (K_B - 1) + k, SUBLANES)
            term = wb_ref[k] * sb[s, pl.ds(base + whole * SUBLANES, CONV_ROWS), :]
            acc = term if acc is None else acc + term
        zb_ref[pl.ds(base, CONV_ROWS), :] = _ln_silu(acc, lg_ref[...], lb_ref[...]).astype(BF16)
        return carry

    lax.fori_loop(0, TC // CONV_ROWS, chunk, 0)


def _conv_prompt(b_a, cx, glu, n_rows, seq, wa, wb, ln_g, ln_b):
    c = cx.shape[1]
    tile = lambda i: (i, 0)
    halo = lambda i: (jnp.maximum(i * (TC // HALO) - 1, 0), 0)
    fixed = lambda i: (0, 0)
    return pl.pallas_call(
        functools.partial(_conv_prompt_body, seq // TC),
        grid=(n_rows // TC,),
        in_specs=[pl.BlockSpec((TC, c), tile), pl.BlockSpec((TC, c), tile), pl.BlockSpec((HALO, c), halo),
                  pl.BlockSpec((TC, c), tile), pl.BlockSpec((HALO, c), halo),
                  pl.BlockSpec((K_A, c), fixed), pl.BlockSpec((K_B, CONV_ROWS, c), lambda i: (0, 0, 0)),
                  pl.BlockSpec((1, c), fixed), pl.BlockSpec((1, c), fixed)],
        out_specs=[pl.BlockSpec((TC, c), tile), pl.BlockSpec((TC, c), tile)],
        out_shape=[jax.ShapeDtypeStruct((n_rows, c), BF16)] * 2,
        scratch_shapes=[pltpu.VMEM((TC + HALO, c), F32), pltpu.VMEM((SUBLANES, TC + HALO, c), F32)],
        compiler_params=_params("parallel"), name="conv_prompt",
    )(b_a, cx, cx, glu, glu, wa, jnp.broadcast_to(wb[:, None, :], (K_B, CONV_ROWS, c)),
      ln_g.reshape(1, c), ln_b.reshape(1, c))


def _conv_sample_body(n_t, ba_ref, ea_ref, eb_ref, wa_ref, wb_ref, lg_ref, lb_ref, za_ref, zb_ref):
    for t in range(n_t):
        acc = None
        for k in range(K_A):
            term = wa_ref[k:k + 1, :] * ea_ref[t + k]
            acc = term if acc is None else acc + term
        za_ref[t] = (ba_ref[t] * acc).astype(BF16)
        acc = None
        for k in range(K_B):
            term = wb_ref[k:k + 1, :] * eb_ref[t + k]
            acc = term if acc is None else acc + term
        zb_ref[t] = _ln_silu(acc, lg_ref[...], lb_ref[...]).astype(BF16)


def _conv_sample(ba_t, ext_a_t, ext_b_t, wa, wb, ln_g, ln_b):
    n_t, nb, c = ba_t.shape
    return pl.pallas_call(
        functools.partial(_conv_sample_body, n_t),
        out_shape=[jax.ShapeDtypeStruct((n_t, nb, c), BF16)] * 2,
        compiler_params=pltpu.CompilerParams(vmem_limit_bytes=VMEM_LIMIT), name="conv_sample",
    )(ba_t, ext_a_t, ext_b_t, wa, wb, ln_g.reshape(1, c), ln_b.reshape(1, c))


def _cumsum_body(x_ref, o_ref, carry):
    @pl.when(pl.program_id(1) == 0)
    def _():
        carry[...] = jnp.zeros_like(carry)

    r = lax.broadcasted_iota(jnp.int32, (CS, CS), 0)
    c = lax.broadcasted_iota(jnp.int32, (CS, CS), 1)
    tri = jnp.where(c <= r, 1.0, 0.0).astype(F32)
    out = jnp.dot(tri, x_ref[...], preferred_element_type=F32, precision=lax.Precision.HIGHEST) + carry[...]
    o_ref[...] = out
    carry[...] = out[CS - 1:CS, :]


def _cumsum_rows(x):
    nb, rows, c = x.shape
    return pl.pallas_call(
        _cumsum_body, grid=(nb, rows // CS),
        in_specs=[pl.BlockSpec((None, CS, c), lambda b, i: (b, i, 0))],
        out_specs=pl.BlockSpec((None, CS, c), lambda b, i: (b, i, 0)),
        out_shape=jax.ShapeDtypeStruct((nb, rows, c), F32),
        scratch_shapes=[pltpu.VMEM((1, c), F32)],
        compiler_params=_params("parallel", "arbitrary"), name="cumsum",
    )(x)


def _fox_prompt_body(nk, q_ref, k_ref, v_ref, cq_ref, ck_ref, o_ref, m_sc, l_sc, acc_sc, cq_sc):
    h = pl.program_id(1)
    i = pl.program_id(2)
    j = pl.program_id(3)

    @pl.when(j == 0)
    def _():
        m_sc[...] = jnp.full_like(m_sc, NEG)
        l_sc[...] = jnp.zeros_like(l_sc)
        acc_sc[...] = jnp.zeros_like(acc_sc)
        lane = lax.broadcasted_iota(jnp.int32, cq_ref.shape, 1)
        cq_sc[...] = jnp.sum(jnp.where(lane == h, cq_ref[...], 0.0), axis=1, keepdims=True)

    def update(on_diagonal):
        s = _dot_nt(q_ref[...], k_ref[...]) * (HD_C ** -0.5)
        s = s + (cq_sc[...] - ck_ref[...])
        if on_diagonal:
            row = lax.broadcasted_iota(jnp.int32, (TQ, TK), 0)
            col = lax.broadcasted_iota(jnp.int32, (TQ, TK), 1)
            s = jnp.where(col <= row, s, -jnp.inf)
        m_new = jnp.maximum(m_sc[...], jnp.max(s, axis=1, keepdims=True))
        alpha = jnp.exp(m_sc[...] - m_new)
        p = jnp.exp(s - m_new)
        l_sc[...] = alpha * l_sc[...] + jnp.sum(p, axis=1, keepdims=True)
        acc_sc[...] = alpha * acc_sc[...] + _dot(p.astype(BF16), v_ref[...])
        m_sc[...] = m_new

    @pl.when(j < i)
    def _():
        update(False)

    @pl.when(j == i)
    def _():
        update(True)

    @pl.when(j == nk - 1)
    def _():
        o_ref[...] = (acc_sc[...] / l_sc[...]).astype(BF16)


def _fox_prompt(qb, kb, vb, c_tok, c_row, nb, seq):
    assert TQ == TK
    nq, nk = seq // TQ, seq // TK
    return pl.pallas_call(
        functools.partial(_fox_prompt_body, nk),
        grid=(nb, H_C, nq, nk),
        in_specs=[pl.BlockSpec((TQ, HD_C), lambda b, h, i, j: (b * nq + i, h)),
                  pl.BlockSpec((TK, HD_C), lambda b, h, i, j: (b * nk + jnp.minimum(j, i), h)),
                  pl.BlockSpec((TK, HD_C), lambda b, h, i, j: (b * nk + jnp.minimum(j, i), h)),
                  pl.BlockSpec((TQ, LANES), lambda b, h, i, j: (b * nq + i, 0)),
                  pl.BlockSpec((None, None, 1, TK), lambda b, h, i, j: (b, h, 0, jnp.minimum(j, i)))],
        out_specs=pl.BlockSpec((TQ, HD_C), lambda b, h, i, j: (b * nq + i, h)),
        out_shape=jax.ShapeDtypeStruct((nb * seq, H_C * HD_C), BF16),
        scratch_shapes=[pltpu.VMEM((TQ, 1), F32), pltpu.VMEM((TQ, 1), F32), pltpu.VMEM((TQ, HD_C), F32),
                        pltpu.VMEM((TQ, 1), F32)],
        compiler_params=_params("parallel", "parallel", "parallel", "arbitrary"), name="fox_prompt",
    )(qb, kb, vb, c_tok, c_row)


def _fox_sample_body(n_steps, pt_ref, q_ref, kn_ref, vn_ref, lfr_ref, lfc_ref, *rest):
    g_pages = PAGES_PER_STEP
    k_refs = rest[:g_pages]
    v_refs = rest[g_pages:2 * g_pages]
    lf_refs = rest[2 * g_pages:3 * g_pages]
    o_ref, m_sc, l_sc, acc_sc, cq_sc, run_sc, lf_sc = rest[3 * g_pages:]
    step = pl.program_id(1)
    n_q = q_ref.shape[0]
    n_key = PAGE * H_C
    scale = HD_C ** -0.5
    q = q_ref[...]

    @pl.when(step == 0)
    def _():
        r = lax.broadcasted_iota(jnp.int32, (n_q, n_q), 0)
        c = lax.broadcasted_iota(jnp.int32, (n_q, n_q), 1)
        ok = ((r % H_C) == (c % H_C)) & (c // H_C <= r // H_C)
        ok_t = ((r % H_C) == (c % H_C)) & (r // H_C <= c // H_C)
        cq_col = jnp.sum(jnp.where(ok, lfr_ref[...], 0.0), axis=1, keepdims=True)
        cq_row = jnp.sum(jnp.where(ok_t, lfc_ref[...], 0.0), axis=0, keepdims=True)
        s = _dot_nt(q, kn_ref[...]) * scale + (cq_col - cq_row)
        s = jnp.where(ok, s, -jnp.inf)
        m = jnp.max(s, axis=1, keepdims=True)
        p = jnp.exp(s - m)
        m_sc[...] = m
        l_sc[...] = jnp.sum(p, axis=1, keepdims=True)
        acc_sc[...] = _dot(p.astype(BF16), vn_ref[...])
        cq_sc[...] = cq_col
        run_sc[...] = jnp.zeros_like(run_sc)

    for g in range(g_pages):
        lf_sc[g:g + 1, :] = lf_refs[g][...]
    lf = lf_sc[...]
    lane = lax.broadcasted_iota(jnp.int32, lf.shape, 1)
    incl = lf
    sh = H_C
    while sh < n_key:
        incl = incl + jnp.where(lane < n_key - sh, pltpu.roll(incl, n_key - sh, axis=1), 0.0)
        sh *= 2
    tot = jnp.where(lane < H_C, incl, 0.0)
    sh = H_C
    while sh < n_key:
        tot = tot + pltpu.roll(tot, sh, axis=1)
        sh *= 2
    after = incl - lf

    r = lax.broadcasted_iota(jnp.int32, (n_q, n_key), 0)
    c = lax.broadcasted_iota(jnp.int32, (n_q, n_key), 1)
    cq_masked = jnp.where((r % H_C) == (c % H_C), cq_sc[...], -jnp.inf)
    run = run_sc[...]
    scores = []
    for g in range(g_pages):
        k2 = k_refs[g][...].reshape(n_key, HD_C).astype(BF16)
        scores.append(_dot_nt(q, k2) * scale + (cq_masked + (after[g:g + 1, :] + run)))
        run = run + tot[g:g + 1, :]
    run_sc[...] = run

    top = scores[0]
    for s in scores[1:]:
        top = jnp.maximum(top, s)
    m_old = m_sc[...]
    m_new = jnp.maximum(m_old, jnp.max(top, axis=1, keepdims=True))
    alpha = jnp.exp(m_old - m_new)
    p_sum = None
    pv = None
    for g in range(g_pages):
        p = jnp.exp(scores[g] - m_new)
        v2 = v_refs[g][...].reshape(n_key, HD_C).astype(BF16)
        term = _dot(p.astype(BF16), v2)
        p_sum = p if p_sum is None else p_sum + p
        pv = term if pv is None else pv + term
    l_new = alpha * l_sc[...] + jnp.sum(p_sum, axis=1, keepdims=True)
    acc_new = alpha * acc_sc[...] + pv
    m_sc[...] = m_new
    l_sc[...] = l_new
    acc_sc[...] = acc_new

    @pl.when(step == n_steps - 1)
    def _():
        o_ref[...] = (acc_new / l_new).astype(BF16)


def _fox_sample(layer, page_table, q, k_new, v_new, lf_row, lf_col, cache_k, cache_v, cache_lf):
    nb, n_q, _ = q.shape
    n_pages = page_table.shape[1]
    g_pages = PAGES_PER_STEP
    n_steps = n_pages // g_pages
    pt = page_table.reshape(-1)

    def page(g, b, s, pt_ref):
        return pt_ref[b * n_pages + (n_pages - 1 - (s * g_pages + g))]

    per_b = lambda b, s, pt_ref: (b, 0, 0)
    kv_specs = [pl.BlockSpec((None, None, PAGE, H_C, HD_C),
                             functools.partial(lambda g, b, s, pt_ref: (layer, page(g, b, s, pt_ref), 0, 0, 0), g))
                for g in range(g_pages)]
    lf_specs = [pl.BlockSpec((None, None, 1, PAGE * H_C),
                             functools.partial(lambda g, b, s, pt_ref: (layer, page(g, b, s, pt_ref), 0, 0), g))
                for g in range(g_pages)]
    grid_spec = pltpu.PrefetchScalarGridSpec(
        num_scalar_prefetch=1, grid=(nb, n_steps),
        in_specs=[pl.BlockSpec((None, n_q, HD_C), per_b), pl.BlockSpec((None, n_q, HD_C), per_b),
                  pl.BlockSpec((None, n_q, HD_C), per_b), pl.BlockSpec((None, 1, n_q), per_b),
                  pl.BlockSpec((None, n_q, 1), per_b)] + kv_specs + kv_specs + lf_specs,
        out_specs=pl.BlockSpec((None, n_q, HD_C), per_b),
        scratch_shapes=[pltpu.VMEM((n_q, 1), F32), pltpu.VMEM((n_q, 1), F32), pltpu.VMEM((n_q, HD_C), F32),
                        pltpu.VMEM((n_q, 1), F32), pltpu.VMEM((1, PAGE * H_C), F32),
                        pltpu.VMEM((g_pages, PAGE * H_C), F32)])
    return pl.pallas_call(
        functools.partial(_fox_sample_body, n_steps), grid_spec=grid_spec,
        out_shape=jax.ShapeDtypeStruct((nb, n_q, HD_C), BF16),
        compiler_params=_params("parallel", "arbitrary"), name="fox_sample",
    )(pt, q, k_new, v_new, lf_row, lf_col, *([cache_k] * g_pages), *([cache_v] * g_pages),
      *([cache_lf] * g_pages))


def _mem_attn_body(q_ref, mk_ref, mv_ref, o_ref):
    q = q_ref[...].astype(BF16)
    for h in range(H_M):
        cols = slice(h * HD_M, (h + 1) * HD_M)
        s = _dot_nt(q[:, cols], mk_ref[:, cols].astype(BF16)) * (HD_M ** -0.5)
        m = jnp.max(s, axis=1, keepdims=True)
        p = jnp.exp(s - m)
        l = jnp.sum(p, axis=1, keepdims=True)
        o = _dot(p.astype(BF16), mv_ref[:, cols].astype(BF16)) / l
        o_ref[:, cols] = o.astype(o_ref.dtype)


def _mem_attn(layer, q, mk, mv, nb, rows_per_b, tq, out_dtype):
    d = q.shape[1]
    nt = rows_per_b // tq
    mem_spec = pl.BlockSpec((None, None, N_MEM, d), lambda b, i: (layer, b, 0, 0))
    return pl.pallas_call(
        _mem_attn_body, grid=(nb, nt),
        in_specs=[pl.BlockSpec((tq, d), lambda b, i: (b * nt + i, 0)), mem_spec, mem_spec],
        out_specs=pl.BlockSpec((tq, d), lambda b, i: (b * nt + i, 0)),
        out_shape=jax.ShapeDtypeStruct((nb * rows_per_b, d), out_dtype),
        compiler_params=_params("parallel", "arbitrary"), name="mem_attn",
    )(q, mk, mv)


def _merge_body(x_ref, h_ref, *refs):
    z_refs = refs[:N_BRANCH]
    wg_refs = refs[N_BRANCH:2 * N_BRANCH]
    w_refs = refs[2 * N_BRANCH:3 * N_BRANCH]
    wo_ref, o_ref = refs[3 * N_BRANCH:]

    @pl.when(pl.program_id(1) == 0)
    def _():
        o_ref[...] = x_ref[...]

    h = h_ref[...]
    merged = None
    for z, wg, w in zip(z_refs, wg_refs, w_refs):
        term = jax.nn.sigmoid(_dot(h, wg[...])) * _dot(z[...], w[...])
        merged = term if merged is None else merged + term
    merged = merged.astype(BF16)
    for c in range(o_ref.shape[1] // DOWN_CHUNK):
        cols = slice(c * DOWN_CHUNK, (c + 1) * DOWN_CHUNK)
        o_ref[:, cols] += _dot(merged, wo_ref[:, cols])


def _merge(layer, x, h, zs, w_all, off_gates, w_outs, w_o):
    rows, d = x.shape
    tn = TN_MERGE
    nn = d // tn
    row = lambda i, n: (i, 0)
    once = pl.Buffered(1)
    in_specs = [pl.BlockSpec((TM_MERGE, d), row, pipeline_mode=once), pl.BlockSpec((TM_MERGE, d), row)]
    in_specs += [pl.BlockSpec((TM_MERGE, D_BR), row, pipeline_mode=once) for _ in range(N_BRANCH)]
    in_specs += [pl.BlockSpec((None, d, tn),
                              functools.partial(lambda blk, i, n: (layer, 0, blk + n), (off_gates + br * d) // tn))
                 for br in range(N_BRANCH)]
    in_specs += [pl.BlockSpec((None, D_BR, tn), lambda i, n: (layer, 0, n)) for _ in range(N_BRANCH)]
    in_specs += [pl.BlockSpec((None, tn, d), lambda i, n: (layer, n, 0))]
    return pl.pallas_call(
        _merge_body, grid=(rows // TM_MERGE, nn), in_specs=in_specs,
        out_specs=pl.BlockSpec((TM_MERGE, d), row),
        out_shape=jax.ShapeDtypeStruct((rows, d), F32),
        compiler_params=_params("parallel", "arbitrary"), name="merge",
    )(x, h, *zs, *([w_all] * N_BRANCH), *w_outs, w_o)


def kernel(x_prompt, x_sample, cache_fox_k, cache_fox_v, cache_fox_lf, cache_mem_k, cache_mem_v, state_conv_a, state_conv_b, page_table, mem_prompt, n_ffn1, w_ffn1_gate, w_ffn1_up, w_ffn1_down, n_mix, w_in, b_forget, conv_a, w_a_out, conv_b, ln_b_g, ln_b_b, w_b_out, qn_c, kn_c, w_c_out, n_mem, w_mem_k, w_mem_v, qn_m, kn_m, w_m_out, w_o, n_ffn2, w_ffn2_gate, w_ffn2_up, w_ffn2_down):
    depth = w_in.shape[0]
    bp, seq, d = x_prompt.shape
    db, dseq, _ = x_sample.shape
    n_p = bp * seq
    n_s = db * dseq
    n_phys = cache_fox_lf.shape[1]

    x = jnp.concatenate([x_prompt.reshape(n_p, d), x_sample.reshape(n_s, d)], axis=0)
    mem_rows = mem_prompt.reshape(bp * N_MEM, d)
    cache_lf_rows = cache_fox_lf.reshape(depth, n_phys, 1, PAGE * H_C)
    cache_mk = cache_mem_k.reshape(depth, db, N_MEM, D_BR)
    cache_mv = cache_mem_v.reshape(depth, db, N_MEM, D_BR)
    w_branch = [w.astype(BF16) for w in (w_a_out, w_b_out, w_c_out, w_m_out)]
    w_o_b = w_o.astype(BF16)

    fox_end = 8 * D_BR
    off_ba, off_ca, off_xa, off_g1, off_g2, off_q, off_k, off_v = [s * D_BR for s in range(8)]
    off_qm = fox_end
    off_gates = fox_end + D_BR
    off_f = off_gates + N_BRANCH * D_MODEL

    w_f = jnp.pad(w_in[:, :, fox_end:fox_end + H_C], ((0, 0), (0, 0), (0, LANES - H_C)))
    w_all = jnp.concatenate([w_in[:, :, :fox_end], w_in[:, :, fox_end + H_C:], w_f], axis=2).astype(BF16)
    w_mkv = jnp.concatenate([w_mem_k, w_mem_v], axis=2).astype(BF16)

    outs = [[] for _ in range(12)]
    for l in range(depth):
        b_f = jnp.pad(b_forget[l], (0, LANES - H_C)).reshape(1, LANES)

        x, h = _ffn(l, x, n_ffn1[l], w_ffn1_gate, w_ffn1_up, w_ffn1_down, n_mix[l])

        b_a, cx = _proj(l, h, w_all, [off_ba, off_ca, off_xa], D_BR, [], _epi_conv_a, [F32, F32],
                        name="proj_conv_a")
        (glu,) = _proj(l, h, w_all, [off_g1, off_g2], D_BR, [], _epi_glu, [F32], name="proj_glu")
        q_c, k_c, kb_c, v_c, vb_c = _proj(
            l, h, w_all, [off_q, off_k, off_v], D_BR, [qn_c[l].reshape(1, HD_C), kn_c[l].reshape(1, HD_C)],
            _epi_fox, [BF16, F32, BF16, F32, BF16], name="proj_fox")
        (lf_pad,) = _proj(l, h, w_all, [off_f], LANES, [b_f], _epi_forget, [F32], tn=LANES, name="proj_forget")
        (q_m,) = _proj(l, h, w_all, [off_qm], D_BR, [qn_m[l].reshape(1, HD_M)], _epi_qmem, [BF16],
                       name="proj_qmem")

        m_h = _rms_cast(mem_rows, n_mem[l], N_MEM)
        mk_p, mv_p = _proj(l, m_h, w_mkv, [0, D_BR], D_BR, [kn_m[l].reshape(1, HD_M)], _epi_memkv, [F32, F32],
                           tm=N_MEM, name="proj_memkv")

        za_p, zb_p = _conv_prompt(b_a, cx, glu, n_p, seq, conv_a[l], conv_b[l], ln_b_g[l], ln_b_b[l])
        cx_s = cx[n_p:].reshape(db, dseq, D_BR)
        glu_s = glu[n_p:].reshape(db, dseq, D_BR)
        ext_a = jnp.concatenate([state_conv_a[l], cx_s], axis=1)
        ext_b = jnp.concatenate([state_conv_b[l], glu_s], axis=1)
        ba_t = jnp.swapaxes(b_a[n_p:].reshape(db, dseq, D_BR), 0, 1)
        za_s, zb_s = _conv_sample(ba_t, jnp.swapaxes(ext_a, 0, 1), jnp.swapaxes(ext_b, 0, 1),
                                  conv_a[l], conv_b[l], ln_b_g[l], ln_b_b[l])
        z_a = jnp.concatenate([za_p, jnp.swapaxes(za_s, 0, 1).reshape(n_s, D_BR)], axis=0)
        z_b = jnp.concatenate([zb_p, jnp.swapaxes(zb_s, 0, 1).reshape(n_s, D_BR)], axis=0)

        lf = lf_pad[:, :H_C]
        c_tok = _cumsum_rows(lf_pad[:n_p].reshape(bp, seq, LANES))
        c_row = jnp.swapaxes(c_tok[:, :, :H_C], 1, 2).reshape(bp, H_C, 1, seq)
        yc_p = _fox_prompt(q_c, kb_c, vb_c, c_tok.reshape(n_p, LANES), c_row, bp, seq)
        lf_s = lf[n_p:].reshape(db, dseq * H_C)
        yc_s = _fox_sample(
            l, page_table, q_c[n_p:].reshape(db, dseq * H_C, HD_C), kb_c[n_p:].reshape(db, dseq * H_C, HD_C),
            vb_c[n_p:].reshape(db, dseq * H_C, HD_C), lf_s.reshape(db, 1, dseq * H_C),
            lf_s.reshape(db, dseq * H_C, 1), cache_fox_k, cache_fox_v, cache_lf_rows)
        y_c = jnp.concatenate([yc_p, yc_s.reshape(n_s, D_BR)], axis=0)

        ym_p = _mem_attn(0, q_m, mk_p.reshape(1, bp, N_MEM, D_BR), mv_p.reshape(1, bp, N_MEM, D_BR),
                         bp, seq, TQ_M, BF16)
        qm_s = jnp.pad(q_m[n_p:].reshape(db, dseq, D_BR).astype(F32), ((0, 0), (0, SUBLANES - dseq), (0, 0)))
        ym_s = _mem_attn(l, qm_s.reshape(db * SUBLANES, D_BR), cache_mk, cache_mv, db, SUBLANES, SUBLANES, F32)
        ym_s = ym_s.reshape(db, SUBLANES, D_BR)[:, :dseq].reshape(n_s, D_BR)
        y_m = jnp.concatenate([ym_p, ym_s.astype(BF16)], axis=0)

        x = _merge(l, x, h, [z_a, z_b, y_c, y_m], w_all, off_gates, w_branch, w_o_b)
        x = _ffn(l, x, n_ffn2[l], w_ffn2_gate, w_ffn2_up, w_ffn2_down)

        per_layer = [
            k_c[:n_p].reshape(bp, seq, H_C, HD_C), v_c[:n_p].reshape(bp, seq, H_C, HD_C),
            lf[:n_p].reshape(bp, seq, H_C),
            k_c[n_p:].reshape(db, dseq, H_C, HD_C), v_c[n_p:].reshape(db, dseq, H_C, HD_C),
            lf[n_p:].reshape(db, dseq, H_C),
            mk_p.reshape(bp, N_MEM, H_M, HD_M), mv_p.reshape(bp, N_MEM, H_M, HD_M),
            cx[:n_p].reshape(bp, seq, D_BR)[:, seq - (K_A - 1):], ext_a[:, dseq:],
            glu[:n_p].reshape(bp, seq, D_BR)[:, seq - (K_B - 1):], ext_b[:, dseq:],
        ]
        for acc, val in zip(outs, per_layer):
            acc.append(val)

    return (x[:n_p].reshape(bp, seq, d), x[n_p:].reshape(db, dseq, d)) + tuple(jnp.stack(o) for o in outs)
```

```python
import functools

import jax
import jax.numpy as jnp
from jax import lax
from jax.experimental import pallas as pl
from jax.experimental.pallas import tpu as pltpu

F32 = jnp.float32
BF16 = jnp.bfloat16

D_MODEL = 2048
D_BR = D_MODEL // 2
K_A = 3
K_B = 31
HD_C = 128
H_C = D_BR // HD_C
H_M = 4
HD_M = D_BR // H_M
N_MEM = 256
N_BRANCH = 4
PAGE = 128
NORM_EPS = 1e-6
LN_EPS = 1e-5

LANES = 128
SUBLANES = 8
VMEM_LIMIT = 56 * 1024 * 1024

TM = 1040
TM_FFN = 1040
TM_MERGE = 640
TF = 256
DOWN_CHUNK = 512
TN = 512
TN_MERGE = 256
TC = 512
HALO = 32
CONV_ROWS = 32
TQ = 1024
TK = 1024
TQ_M = 512
CS = 256
PAGES_PER_STEP = 16
NEG = -1e30


def _params(*sem):
    return pltpu.CompilerParams(dimension_semantics=sem, vmem_limit_bytes=VMEM_LIMIT)


def _rms_rows(x, g):
    ms = jnp.mean(x * x, axis=-1, keepdims=True)
    return x * lax.rsqrt(ms + NORM_EPS) * g


def _log_sigmoid(x):
    return jnp.minimum(x, 0.0) - jnp.log1p(jnp.exp(-jnp.abs(x)))


def _dot(a, b):
    return jnp.dot(a, b, preferred_element_type=F32)


def _dot_nt(a, b):
    return lax.dot_general(a, b, (((1,), (1,)), ((), ())), preferred_element_type=F32)


def _ffn_body(emit_h, nf, x_ref, n1_ref, wg_ref, wu_ref, wd_ref, *rest):
    if emit_h:
        n2_ref, o_ref, h2_ref, h_sc = rest
    else:
        o_ref, h_sc = rest
    f = pl.program_id(1)

    @pl.when(f == 0)
    def _():
        x = x_ref[...]
        h_sc[...] = _rms_rows(x, n1_ref[...]).astype(BF16)
        o_ref[...] = x

    h = h_sc[...]
    g = _dot(h, wg_ref[...].astype(BF16))
    u = _dot(h, wu_ref[...].astype(BF16))
    a = (g * jax.nn.sigmoid(g) * u).astype(BF16)
    for c in range(o_ref.shape[1] // DOWN_CHUNK):
        cols = slice(c * DOWN_CHUNK, (c + 1) * DOWN_CHUNK)
        o_ref[:, cols] += 0.5 * _dot(a, wd_ref[:, cols].astype(BF16))

    if emit_h:
        @pl.when(f == nf - 1)
        def _():
            h2_ref[...] = _rms_rows(o_ref[...], n2_ref[...]).astype(BF16)


def _ffn(layer, x, n1, wg, wu, wd, n2=None):
    rows, d = x.shape
    dff = wg.shape[2]
    nf = dff // TF
    emit_h = n2 is not None
    row = lambda i, f: (i, 0)
    fixed = lambda i, f: (0, 0)
    x_spec = pl.BlockSpec((TM_FFN, d), row, pipeline_mode=pl.Buffered(1)) if emit_h else pl.BlockSpec((TM_FFN, d), row)
    in_specs = [x_spec, pl.BlockSpec((1, d), fixed),
                pl.BlockSpec((None, d, TF), lambda i, f: (layer, 0, f)),
                pl.BlockSpec((None, d, TF), lambda i, f: (layer, 0, f)),
                pl.BlockSpec((None, TF, d), lambda i, f: (layer, f, 0))]
    args = [x, n1.reshape(1, d), wg, wu, wd]
    out_shape = [jax.ShapeDtypeStruct((rows, d), F32)]
    out_specs = [pl.BlockSpec((TM_FFN, d), row)]
    if emit_h:
        in_specs.append(pl.BlockSpec((1, d), fixed))
        args.append(n2.reshape(1, d))
        out_shape.append(jax.ShapeDtypeStruct((rows, d), BF16))
        out_specs.append(pl.BlockSpec((TM_FFN, d), row))
    res = pl.pallas_call(
        functools.partial(_ffn_body, emit_h, nf),
        grid=(rows // TM_FFN, nf), in_specs=in_specs, out_specs=out_specs, out_shape=out_shape,
        scratch_shapes=[pltpu.VMEM((TM_FFN, d), BF16)],
        compiler_params=_params("parallel", "arbitrary"), name="ffn_h" if emit_h else "ffn",
    )(*args)
    return res if emit_h else res[0]


def _proj_body(n_w, n_extra, epilogue, h_ref, *refs):
    w_refs = refs[:n_w]
    extra = refs[n_w:n_w + n_extra]
    outs = refs[n_w + n_extra:]
    h = h_ref[...]
    epilogue([_dot(h, w[...]) for w in w_refs], extra, outs)


def _proj(layer, h, w, col_offsets, width, extras, epilogue, out_dtypes, tn=TN, tm=TM, name="proj"):
    rows, d = h.shape
    w_specs = [pl.BlockSpec((None, d, tn), functools.partial(lambda off, i, j: (layer, 0, off + j), off // tn))
               for off in col_offsets]
    e_specs = [pl.BlockSpec(e.shape, lambda i, j: (0, 0)) for e in extras]
    return pl.pallas_call(
        functools.partial(_proj_body, len(col_offsets), len(extras), epilogue),
        grid=(rows // tm, width // tn),
        in_specs=[pl.BlockSpec((tm, d), lambda i, j: (i, 0))] + w_specs + e_specs,
        out_specs=[pl.BlockSpec((tm, tn), lambda i, j: (i, j)) for _ in out_dtypes],
        out_shape=[jax.ShapeDtypeStruct((rows, width), dt) for dt in out_dtypes],
        compiler_params=_params("parallel", "arbitrary"), name=name,
    )(h, *([w] * len(col_offsets)), *extras)


def _head_rms(x, g, hd):
    parts = []
    for c in range(x.shape[1] // hd):
        parts.append(_rms_rows(x[:, c * hd:(c + 1) * hd], g))
    return parts


def _epi_conv_a(accs, extra, outs):
    outs[0][...] = accs[0]
    outs[1][...] = accs[1] * accs[2]


def _epi_glu(accs, extra, outs):
    outs[0][...] = accs[0] * jax.nn.sigmoid(accs[1])


def _epi_fox(accs, extra, outs):
    qn_ref, kn_ref = extra
    q_ref, k_ref, kb_ref, v_ref, vb_ref = outs
    for c, part in enumerate(_head_rms(accs[0], qn_ref[...], HD_C)):
        q_ref[:, c * HD_C:(c + 1) * HD_C] = part.astype(BF16)
    for c, part in enumerate(_head_rms(accs[1], kn_ref[...], HD_C)):
        k_ref[:, c * HD_C:(c + 1) * HD_C] = part
        kb_ref[:, c * HD_C:(c + 1) * HD_C] = part.astype(BF16)
    v_ref[...] = accs[2]
    vb_ref[...] = accs[2].astype(BF16)


def _epi_forget(accs, extra, outs):
    outs[0][...] = _log_sigmoid(accs[0] + extra[0][...])


def _epi_qmem(accs, extra, outs):
    for c, part in enumerate(_head_rms(accs[0], extra[0][...], HD_M)):
        outs[0][:, c * HD_M:(c + 1) * HD_M] = part.astype(BF16)


def _epi_memkv(accs, extra, outs):
    for c, part in enumerate(_head_rms(accs[0], extra[0][...], HD_M)):
        outs[0][:, c * HD_M:(c + 1) * HD_M] = part
    outs[1][...] = accs[1]


def _rms_cast_body(x_ref, g_ref, o_ref):
    o_ref[...] = _rms_rows(x_ref[...], g_ref[...]).astype(BF16)


def _rms_cast(x, g, tm):
    rows, d = x.shape
    return pl.pallas_call(
        _rms_cast_body, grid=(rows // tm,),
        in_specs=[pl.BlockSpec((tm, d), lambda i: (i, 0)), pl.BlockSpec((1, d), lambda i: (0, 0))],
        out_specs=pl.BlockSpec((tm, d), lambda i: (i, 0)),
        out_shape=jax.ShapeDtypeStruct((rows, d), BF16),
        compiler_params=_params("parallel"), name="rms_cast",
    )(x, g.reshape(1, d))


def _ln_silu(x, g, b):
    mu = jnp.mean(x, axis=-1, keepdims=True)
    xc = x - mu
    var = jnp.mean(xc * xc, axis=-1, keepdims=True)
    z = xc * lax.rsqrt(var + LN_EPS) * g + b
    return z * jax.nn.sigmoid(z)


def _conv_prompt_body(tiles_per_seq, ba_ref, cx_ref, cxh_ref, gl_ref, glh_ref, wa_ref, wb_ref,
                      lg_ref, lb_ref, za_ref, zb_ref, sa, sb):
    first = (pl.program_id(0) % tiles_per_seq) == 0
    sa[0:HALO, :] = jnp.where(first, 0.0, cxh_ref[...])
    sa[HALO:, :] = cx_ref[...]
    sb[0, 0:HALO, :] = jnp.where(first, 0.0, glh_ref[...])
    sb[0, HALO:, :] = gl_ref[...]
    acc = None
    for k in range(K_A):
        term = wa_ref[k:k + 1, :] * sa[pl.ds(HALO - (K_A - 1) + k, TC), :]
        acc = term if acc is None else acc + term
    za_ref[...] = (ba_ref[...] * acc).astype(BF16)

    n_shift = TC + HALO - SUBLANES
    for s in range(1, SUBLANES):
        sb[s, 0:n_shift, :] = sb[0, pl.ds(s, n_shift), :]

    def chunk(ci, carry):
        base = pl.multiple_of(ci * CONV_ROWS, CONV_ROWS)
        acc = None
        for k in range(K_B):
            whole, s = divmod(HALO - (K_B - 1) + k, SUBLANES)
            rows = sb[s, pl.ds(base + whole * SUBLANES, CONV_ROWS), :]
            term = wb_ref[k][None] * rows.reshape(CONV_ROWS // SUBLANES, SUBLANES, rows.shape[1])
            acc = term if acc is None else acc + term
        acc = acc.reshape(CONV_ROWS, acc.shape[2])
        zb_ref[pl.ds(base, CONV_ROWS), :] = _ln_silu(acc, lg_ref[...], lb_ref[...]).astype(BF16)
        return carry

    lax.fori_loop(0, TC // CONV_ROWS, chunk, 0)


def _conv_prompt(b_a, cx, glu, n_rows, seq, wa, wb, ln_g, ln_b):
    c = cx.shape[1]
    tile = lambda i: (i, 0)
    halo = lambda i: (jnp.maximum(i * (TC // HALO) - 1, 0), 0)
    fixed = lambda i: (0, 0)
    return pl.pallas_call(
        functools.partial(_conv_prompt_body, seq // TC),
        grid=(n_rows // TC,),
        in_specs=[pl.BlockSpec((TC, c), tile), pl.BlockSpec((TC, c), tile), pl.BlockSpec((HALO, c), halo),
                  pl.BlockSpec((TC, c), tile), pl.BlockSpec((HALO, c), halo),
                  pl.BlockSpec((K_A, c), fixed), pl.BlockSpec((K_B, SUBLANES, c), lambda i: (0, 0, 0)),
                  pl.BlockSpec((1, c), fixed), pl.BlockSpec((1, c), fixed)],
        out_specs=[pl.BlockSpec((TC, c), tile), pl.BlockSpec((TC, c), tile)],
        out_shape=[jax.ShapeDtypeStruct((n_rows, c), BF16)] * 2,
        scratch_shapes=[pltpu.VMEM((TC + HALO, c), F32), pltpu.VMEM((SUBLANES, TC + HALO, c), F32)],
        compiler_params=_params("parallel"), name="conv_prompt",
    )(b_a, cx, cx, glu, glu, wa, jnp.broadcast_to(wb[:, None, :], (K_B, SUBLANES, c)),
      ln_g.reshape(1, c), ln_b.reshape(1, c))


def _conv_sample_body(n_t, ba_ref, ea_ref, eb_ref, wa_ref, wb_ref, lg_ref, lb_ref, za_ref, zb_ref):
    for t in range(n_t):
        acc = None
        for k in range(K_A):
            term = wa_ref[k:k + 1, :] * ea_ref[t + k]
            acc = term if acc is None else acc + term
        za_ref[t] = (ba_ref[t] * acc).astype(BF16)
        acc = None
        for k in range(K_B):
            term = wb_ref[k:k + 1, :] * eb_ref[t + k]
            acc = term if acc is None else acc + term
        zb_ref[t] = _ln_silu(acc, lg_ref[...], lb_ref[...]).astype(BF16)


def _conv_sample(ba_t, ext_a_t, ext_b_t, wa, wb, ln_g, ln_b):
    n_t, nb, c = ba_t.shape
    return pl.pallas_call(
        functools.partial(_conv_sample_body, n_t),
        out_shape=[jax.ShapeDtypeStruct((n_t, nb, c), BF16)] * 2,
        compiler_params=pltpu.CompilerParams(vmem_limit_bytes=VMEM_LIMIT), name="conv_sample",
    )(ba_t, ext_a_t, ext_b_t, wa, wb, ln_g.reshape(1, c), ln_b.reshape(1, c))


def _cumsum_body(x_ref, o_ref, carry):
    @pl.when(pl.program_id(1) == 0)
    def _():
        carry[...] = jnp.zeros_like(carry)

    r = lax.broadcasted_iota(jnp.int32, (CS, CS), 0)
    c = lax.broadcasted_iota(jnp.int32, (CS, CS), 1)
    tri = jnp.where(c <= r, 1.0, 0.0).astype(F32)
    out = jnp.dot(tri, x_ref[...], preferred_element_type=F32, precision=lax.Precision.HIGHEST) + carry[...]
    o_ref[...] = out
    carry[...] = out[CS - 1:CS, :]


def _cumsum_rows(x):
    nb, rows, c = x.shape
    return pl.pallas_call(
        _cumsum_body, grid=(nb, rows // CS),
        in_specs=[pl.BlockSpec((None, CS, c), lambda b, i: (b, i, 0))],
        out_specs=pl.BlockSpec((None, CS, c), lambda b, i: (b, i, 0)),
        out_shape=jax.ShapeDtypeStruct((nb, rows, c), F32),
        scratch_shapes=[pltpu.VMEM((1, c), F32)],
        compiler_params=_params("parallel", "arbitrary"), name="cumsum",
    )(x)


def _fox_prompt_body(qt_ref, kt_ref, q_ref, k_ref, v_ref, cq_ref, ck_ref, o_ref, m_sc, l_sc, acc_sc, cq_sc):
    h = pl.program_id(1)
    i = qt_ref[pl.program_id(2)]
    j = kt_ref[pl.program_id(2)]

    @pl.when(j == 0)
    def _():
        m_sc[...] = jnp.full_like(m_sc, NEG)
        l_sc[...] = jnp.zeros_like(l_sc)
        acc_sc[...] = jnp.zeros_like(acc_sc)
        lane = lax.broadcasted_iota(jnp.int32, cq_ref.shape, 1)
        cq_sc[...] = jnp.sum(jnp.where(lane == h, cq_ref[...], 0.0), axis=1, keepdims=True)

    def update(on_diagonal):
        s = _dot_nt(q_ref[...], k_ref[...]) * (HD_C ** -0.5)
        s = s + (cq_sc[...] - ck_ref[...])
        if on_diagonal:
            row = lax.broadcasted_iota(jnp.int32, (TQ, TK), 0)
            col = lax.broadcasted_iota(jnp.int32, (TQ, TK), 1)
            s = jnp.where(col <= row, s, -jnp.inf)
        m_new = jnp.maximum(m_sc[...], jnp.max(s, axis=1, keepdims=True))
        alpha = jnp.exp(m_sc[...] - m_new)
        p = jnp.exp(s - m_new)
        l_sc[...] = alpha * l_sc[...] + jnp.sum(p, axis=1, keepdims=True)
        acc_sc[...] = alpha * acc_sc[...] + _dot(p.astype(BF16), v_ref[...])
        m_sc[...] = m_new

    @pl.when(j < i)
    def _():
        update(False)

    @pl.when(j == i)
    def _():
        update(True)
        o_ref[...] = (acc_sc[...] / l_sc[...]).astype(BF16)


def _fox_prompt(qb, kb, vb, c_tok, c_row, nb, seq):
    assert TQ == TK
    nt = seq // TQ
    pairs = [(i, j) for i in range(nt) for j in range(i + 1)]
    q_tiles = jnp.array([p[0] for p in pairs], jnp.int32)
    k_tiles = jnp.array([p[1] for p in pairs], jnp.int32)
    q_map = lambda b, h, p, qt, kt: (b * nt + qt[p], h)
    k_map = lambda b, h, p, qt, kt: (b * nt + kt[p], h)
    grid_spec = pltpu.PrefetchScalarGridSpec(
        num_scalar_prefetch=2, grid=(nb, H_C, len(pairs)),
        in_specs=[pl.BlockSpec((TQ, HD_C), q_map), pl.BlockSpec((TK, HD_C), k_map), pl.BlockSpec((TK, HD_C), k_map),
                  pl.BlockSpec((TQ, LANES), lambda b, h, p, qt, kt: (b * nt + qt[p], 0)),
                  pl.BlockSpec((None, None, 1, TK), lambda b, h, p, qt, kt: (b, h, 0, kt[p]))],
        out_specs=pl.BlockSpec((TQ, HD_C), q_map),
        scratch_shapes=[pltpu.VMEM((TQ, 1), F32), pltpu.VMEM((TQ, 1), F32), pltpu.VMEM((TQ, HD_C), F32),
                        pltpu.VMEM((TQ, 1), F32)])
    return pl.pallas_call(
        _fox_prompt_body, grid_spec=grid_spec,
        out_shape=jax.ShapeDtypeStruct((nb * seq, H_C * HD_C), BF16),
        compiler_params=_params("parallel", "parallel", "arbitrary"), name="fox_prompt",
    )(q_tiles, k_tiles, qb, kb, vb, c_tok, c_row)


def _fox_sample_body(n_steps, pt_ref, q_ref, kn_ref, vn_ref, lfr_ref, lfc_ref, *rest):
    g_pages = PAGES_PER_STEP
    k_refs = rest[:g_pages]
    v_refs = rest[g_pages:2 * g_pages]
    lf_refs = rest[2 * g_pages:3 * g_pages]
    o_ref, m_sc, l_sc, acc_sc, cq_sc, run_sc, lf_sc = rest[3 * g_pages:]
    step = pl.program_id(1)
    n_q = q_ref.shape[0]
    n_key = PAGE * H_C
    scale = HD_C ** -0.5
    q = q_ref[...]

    @pl.when(step == 0)
    def _():
        r = lax.broadcasted_iota(jnp.int32, (n_q, n_q), 0)
        c = lax.broadcasted_iota(jnp.int32, (n_q, n_q), 1)
        ok = ((r % H_C) == (c % H_C)) & (c // H_C <= r // H_C)
        ok_t = ((r % H_C) == (c % H_C)) & (r // H_C <= c // H_C)
        cq_col = jnp.sum(jnp.where(ok, lfr_ref[...], 0.0), axis=1, keepdims=True)
        cq_row = jnp.sum(jnp.where(ok_t, lfc_ref[...], 0.0), axis=0, keepdims=True)
        s = _dot_nt(q, kn_ref[...]) * scale + (cq_col - cq_row)
        s = jnp.where(ok, s, -jnp.inf)
        m = jnp.max(s, axis=1, keepdims=True)
        p = jnp.exp(s - m)
        m_sc[...] = m
        l_sc[...] = jnp.sum(p, axis=1, keepdims=True)
        acc_sc[...] = _dot(p.astype(BF16), vn_ref[...])
        cq_sc[...] = cq_col
        run_sc[...] = jnp.zeros_like(run_sc)

    for g in range(g_pages):
        lf_sc[g:g + 1, :] = lf_refs[g][...]
    lf = lf_sc[...]
    lane = lax.broadcasted_iota(jnp.int32, lf.shape, 1)
    incl = lf
    sh = H_C
    while sh < n_key:
        incl = incl + jnp.where(lane < n_key - sh, pltpu.roll(incl, n_key - sh, axis=1), 0.0)
        sh *= 2
    tot = jnp.where(lane < H_C, incl, 0.0)
    sh = H_C
    while sh < n_key:
        tot = tot + pltpu.roll(tot, sh, axis=1)
        sh *= 2
    after = incl - lf

    r = lax.broadcasted_iota(jnp.int32, (n_q, n_key), 0)
    c = lax.broadcasted_iota(jnp.int32, (n_q, n_key), 1)
    cq_masked = jnp.where((r % H_C) == (c % H_C), cq_sc[...], -jnp.inf)
    run = run_sc[...]
    scores = []
    for g in range(g_pages):
        k2 = k_refs[g][...].reshape(n_key, HD_C).astype(BF16)
        scores.append(_dot_nt(q, k2) * scale + (cq_masked + (after[g:g + 1, :] + run)))
        run = run + tot[g:g + 1, :]
    run_sc[...] = run

    top = scores[0]
    for s in scores[1:]:
        top = jnp.maximum(top, s)
    m_old = m_sc[...]
    m_new = jnp.maximum(m_old, jnp.max(top, axis=1, keepdims=True))
    alpha = jnp.exp(m_old - m_new)
    p_sum = None
    pv = None
    for g in range(g_pages):
        p = jnp.exp(scores[g] - m_new)
        v2 = v_refs[g][...].reshape(n_key, HD_C).astype(BF16)
        term = _dot(p.astype(BF16), v2)
        p_sum = p if p_sum is None else p_sum + p
        pv = term if pv is None else pv + term
    l_new = alpha * l_sc[...] + jnp.sum(p_sum, axis=1, keepdims=True)
    acc_new = alpha * acc_sc[...] + pv
    m_sc[...] = m_new
    l_sc[...] = l_new
    acc_sc[...] = acc_new

    @pl.when(step == n_steps - 1)
    def _():
        o_ref[...] = (acc_new / l_new).astype(BF16)


def _fox_sample(layer, page_table, q, k_new, v_new, lf_row, lf_col, cache_k, cache_v, cache_lf):
    nb, n_q, _ = q.shape
    n_pages = page_table.shape[1]
    g_pages = PAGES_PER_STEP
    n_steps = n_pages // g_pages
    pt = page_table.reshape(-1)

    def page(g, b, s, pt_ref):
        return pt_ref[b * n_pages + (n_pages - 1 - (s * g_pages + g))]

    per_b = lambda b, s, pt_ref: (b, 0, 0)
    kv_specs = [pl.BlockSpec((None, None, PAGE, H_C, HD_C),
                             functools.partial(lambda g, b, s, pt_ref: (layer, page(g, b, s, pt_ref), 0, 0, 0), g))
                for g in range(g_pages)]
    lf_specs = [pl.BlockSpec((None, None, 1, PAGE * H_C),
                             functools.partial(lambda g, b, s, pt_ref: (layer, page(g, b, s, pt_ref), 0, 0), g))
                for g in range(g_pages)]
    grid_spec = pltpu.PrefetchScalarGridSpec(
        num_scalar_prefetch=1, grid=(nb, n_steps),
        in_specs=[pl.BlockSpec((None, n_q, HD_C), per_b), pl.BlockSpec((None, n_q, HD_C), per_b),
                  pl.BlockSpec((None, n_q, HD_C), per_b), pl.BlockSpec((None, 1, n_q), per_b),
                  pl.BlockSpec((None, n_q, 1), per_b)] + kv_specs + kv_specs + lf_specs,
        out_specs=pl.BlockSpec((None, n_q, HD_C), per_b),
        scratch_shapes=[pltpu.VMEM((n_q, 1), F32), pltpu.VMEM((n_q, 1), F32), pltpu.VMEM((n_q, HD_C), F32),
                        pltpu.VMEM((n_q, 1), F32), pltpu.VMEM((1, PAGE * H_C), F32),
                        pltpu.VMEM((g_pages, PAGE * H_C), F32)])
    return pl.pallas_call(
        functools.partial(_fox_sample_body, n_steps), grid_spec=grid_spec,
        out_shape=jax.ShapeDtypeStruct((nb, n_q, HD_C), BF16),
        compiler_params=_params("parallel", "arbitrary"), name="fox_sample",
    )(pt, q, k_new, v_new, lf_row, lf_col, *([cache_k] * g_pages), *([cache_v] * g_pages),
      *([cache_lf] * g_pages))


def _mem_attn_body(heads_split, q_ref, mk_ref, mv_ref, o_ref):
    q = q_ref[...].astype(BF16)
    for h in range(H_M):
        cols = slice(h * HD_M, (h + 1) * HD_M)
        mk = mk_ref[:, h, :] if heads_split else mk_ref[:, cols]
        mv = mv_ref[:, h, :] if heads_split else mv_ref[:, cols]
        s = _dot_nt(q[:, cols], mk.astype(BF16)) * (HD_M ** -0.5)
        m = jnp.max(s, axis=1, keepdims=True)
        p = jnp.exp(s - m)
        l = jnp.sum(p, axis=1, keepdims=True)
        o = _dot(p.astype(BF16), mv.astype(BF16)) / l
        o_ref[:, cols] = o.astype(o_ref.dtype)


def _mem_attn(layer, q, mk, mv, nb, rows_per_b, tq, out_dtype):
    d = q.shape[1]
    nt = rows_per_b // tq
    heads_split = mk.ndim == 5
    if heads_split:
        mem_spec = pl.BlockSpec((None, None, N_MEM, H_M, HD_M), lambda b, i: (layer, b, 0, 0, 0))
    else:
        mem_spec = pl.BlockSpec((None, None, N_MEM, d), lambda b, i: (layer, b, 0, 0))
    return pl.pallas_call(
        functools.partial(_mem_attn_body, heads_split), grid=(nb, nt),
        in_specs=[pl.BlockSpec((tq, d), lambda b, i: (b * nt + i, 0)), mem_spec, mem_spec],
        out_specs=pl.BlockSpec((tq, d), lambda b, i: (b * nt + i, 0)),
        out_shape=jax.ShapeDtypeStruct((nb * rows_per_b, d), out_dtype),
        compiler_params=_params("parallel", "arbitrary"), name="mem_attn",
    )(q, mk, mv)


def _merge_body(x_ref, h_ref, *refs):
    z_refs = refs[:N_BRANCH]
    wg_ref, w_ref, wo_ref, o_ref = refs[N_BRANCH:]
    tn = w_ref.shape[2]

    @pl.when(pl.program_id(1) == 0)
    def _():
        o_ref[...] = x_ref[...]

    gates = jax.nn.sigmoid(_dot(h_ref[...], wg_ref[...]))
    merged = None
    for br, z in enumerate(z_refs):
        term = gates[:, br * tn:(br + 1) * tn] * _dot(z[...], w_ref[br])
        merged = term if merged is None else merged + term
    merged = merged.astype(BF16)
    for c in range(o_ref.shape[1] // DOWN_CHUNK):
        cols = slice(c * DOWN_CHUNK, (c + 1) * DOWN_CHUNK)
        o_ref[:, cols] += _dot(merged, wo_ref[:, cols])


def _merge(layer, x, h, zs, w_all, off_gates, w_outs, w_o):
    rows, d = x.shape
    tn = TN_MERGE
    gate_blk = off_gates // (N_BRANCH * tn)
    row = lambda i, n: (i, 0)
    once = pl.Buffered(1)
    in_specs = [pl.BlockSpec((TM_MERGE, d), row, pipeline_mode=once), pl.BlockSpec((TM_MERGE, d), row)]
    in_specs += [pl.BlockSpec((TM_MERGE, D_BR), row, pipeline_mode=once) for _ in range(N_BRANCH)]
    in_specs += [pl.BlockSpec((None, d, N_BRANCH * tn), lambda i, n: (layer, 0, gate_blk + n)),
                 pl.BlockSpec((None, N_BRANCH, D_BR, tn), lambda i, n: (layer, 0, 0, n)),
                 pl.BlockSpec((None, tn, d), lambda i, n: (layer, n, 0))]
    return pl.pallas_call(
        _merge_body, grid=(rows // TM_MERGE, d // tn), in_specs=in_specs,
        out_specs=pl.BlockSpec((TM_MERGE, d), row),
        out_shape=jax.ShapeDtypeStruct((rows, d), F32),
        compiler_params=_params("parallel", "arbitrary"), name="merge",
    )(x, h, *zs, w_all, w_outs, w_o)


def kernel(x_prompt, x_sample, cache_fox_k, cache_fox_v, cache_fox_lf, cache_mem_k, cache_mem_v, state_conv_a, state_conv_b, page_table, mem_prompt, n_ffn1, w_ffn1_gate, w_ffn1_up, w_ffn1_down, n_mix, w_in, b_forget, conv_a, w_a_out, conv_b, ln_b_g, ln_b_b, w_b_out, qn_c, kn_c, w_c_out, n_mem, w_mem_k, w_mem_v, qn_m, kn_m, w_m_out, w_o, n_ffn2, w_ffn2_gate, w_ffn2_up, w_ffn2_down):
    depth = w_in.shape[0]
    bp, seq, d = x_prompt.shape
    db, dseq, _ = x_sample.shape
    n_p = bp * seq
    n_s = db * dseq
    n_phys = cache_fox_lf.shape[1]

    x = jnp.concatenate([x_prompt.reshape(n_p, d), x_sample.reshape(n_s, d)], axis=0)
    mem_rows = mem_prompt.reshape(bp * N_MEM, d)
    cache_lf_rows = cache_fox_lf.reshape(depth, n_phys, 1, PAGE * H_C)
    w_branch = jnp.stack([w_a_out, w_b_out, w_c_out, w_m_out], axis=1).astype(BF16)
    w_o_b = w_o.astype(BF16)

    fox_end = 8 * D_BR
    off_ba, off_ca, off_xa, off_g1, off_g2, off_q, off_k, off_v = [s * D_BR for s in range(8)]
    off_qm = fox_end
    off_gates = fox_end + D_BR
    off_f = off_gates + N_BRANCH * D_MODEL

    w_f = jnp.pad(w_in[:, :, fox_end:fox_end + H_C], ((0, 0), (0, 0), (0, LANES - H_C)))
    gate_start = fox_end + H_C + D_BR
    w_gates = w_in[:, :, gate_start:].reshape(depth, d, N_BRANCH, d // TN_MERGE, TN_MERGE)
    w_gates = jnp.swapaxes(w_gates, 2, 3).reshape(depth, d, N_BRANCH * d)
    w_all = jnp.concatenate([w_in[:, :, :fox_end], w_in[:, :, fox_end + H_C:gate_start], w_gates, w_f],
                            axis=2).astype(BF16)
    w_mkv = jnp.concatenate([w_mem_k, w_mem_v], axis=2).astype(BF16)

    outs = [[] for _ in range(12)]
    for l in range(depth):
        b_f = jnp.pad(b_forget[l], (0, LANES - H_C)).reshape(1, LANES)

        x, h = _ffn(l, x, n_ffn1[l], w_ffn1_gate, w_ffn1_up, w_ffn1_down, n_mix[l])

        b_a, cx = _proj(l, h, w_all, [off_ba, off_ca, off_xa], D_BR, [], _epi_conv_a, [F32, F32],
                        name="proj_conv_a")
        (glu,) = _proj(l, h, w_all, [off_g1, off_g2], D_BR, [], _epi_glu, [F32], name="proj_glu")
        q_c, k_c, kb_c, v_c, vb_c = _proj(
            l, h, w_all, [off_q, off_k, off_v], D_BR, [qn_c[l].reshape(1, HD_C), kn_c[l].reshape(1, HD_C)],
            _epi_fox, [BF16, F32, BF16, F32, BF16], name="proj_fox")
        (lf_pad,) = _proj(l, h, w_all, [off_f], LANES, [b_f], _epi_forget, [F32], tn=LANES, name="proj_forget")
        (q_m,) = _proj(l, h, w_all, [off_qm], D_BR, [qn_m[l].reshape(1, HD_M)], _epi_qmem, [BF16],
                       name="proj_qmem")

        m_h = _rms_cast(mem_rows, n_mem[l], N_MEM)
        mk_p, mv_p = _proj(l, m_h, w_mkv, [0, D_BR], D_BR, [kn_m[l].reshape(1, HD_M)], _epi_memkv, [F32, F32],
                           tm=N_MEM, name="proj_memkv")

        za_p, zb_p = _conv_prompt(b_a, cx, glu, n_p, seq, conv_a[l], conv_b[l], ln_b_g[l], ln_b_b[l])
        cx_s = cx[n_p:].reshape(db, dseq, D_BR)
        glu_s = glu[n_p:].reshape(db, dseq, D_BR)
        ext_a = jnp.concatenate([state_conv_a[l], cx_s], axis=1)
        ext_b = jnp.concatenate([state_conv_b[l], glu_s], axis=1)
        ba_t = jnp.swapaxes(b_a[n_p:].reshape(db, dseq, D_BR), 0, 1)
        za_s, zb_s = _conv_sample(ba_t, jnp.swapaxes(ext_a, 0, 1), jnp.swapaxes(ext_b, 0, 1),
                                  conv_a[l], conv_b[l], ln_b_g[l], ln_b_b[l])
        z_a = jnp.concatenate([za_p, jnp.swapaxes(za_s, 0, 1).reshape(n_s, D_BR)], axis=0)
        z_b = jnp.concatenate([zb_p, jnp.swapaxes(zb_s, 0, 1).reshape(n_s, D_BR)], axis=0)

        lf = lf_pad[:, :H_C]
        c_tok = _cumsum_rows(lf_pad[:n_p].reshape(bp, seq, LANES))
        c_row = jnp.swapaxes(c_tok[:, :, :H_C], 1, 2).reshape(bp, H_C, 1, seq)
        yc_p = _fox_prompt(q_c, kb_c, vb_c, c_tok.reshape(n_p, LANES), c_row, bp, seq)
        lf_s = lf[n_p:].reshape(db, dseq * H_C)
        yc_s = _fox_sample(
            l, page_table, q_c[n_p:].reshape(db, dseq * H_C, HD_C), kb_c[n_p:].reshape(db, dseq * H_C, HD_C),
            vb_c[n_p:].reshape(db, dseq * H_C, HD_C), lf_s.reshape(db, 1, dseq * H_C),
            lf_s.reshape(db, dseq * H_C, 1), cache_fox_k, cache_fox_v, cache_lf_rows)
        y_c = jnp.concatenate([yc_p, yc_s.reshape(n_s, D_BR)], axis=0)

        ym_p = _mem_attn(0, q_m, mk_p.reshape(1, bp, N_MEM, D_BR), mv_p.reshape(1, bp, N_MEM, D_BR),
                         bp, seq, TQ_M, BF16)
        qm_s = jnp.pad(q_m[n_p:].reshape(db, dseq, D_BR).astype(F32), ((0, 0), (0, SUBLANES - dseq), (0, 0)))
        ym_s = _mem_attn(l, qm_s.reshape(db * SUBLANES, D_BR), cache_mem_k, cache_mem_v, db, SUBLANES, SUBLANES,
                         F32)
        ym_s = ym_s.reshape(db, SUBLANES, D_BR)[:, :dseq].reshape(n_s, D_BR)
        y_m = jnp.concatenate([ym_p, ym_s.astype(BF16)], axis=0)

        x = _merge(l, x, h, [z_a, z_b, y_c, y_m], w_all, off_gates, w_branch, w_o_b)
        x = _ffn(l, x, n_ffn2[l], w_ffn2_gate, w_ffn2_up, w_ffn2_down)

        per_layer = [
            k_c[:n_p].reshape(bp, seq, H_C, HD_C), v_c[:n_p].reshape(bp, seq, H_C, HD_C),
            lf[:n_p].reshape(bp, seq, H_C),
            k_c[n_p:].reshape(db, dseq, H_C, HD_C), v_c[n_p:].reshape(db, dseq, H_C, HD_C),
            lf[n_p:].reshape(db, dseq, H_C),
            mk_p.reshape(bp, N_MEM, H_M, HD_M), mv_p.reshape(bp, N_MEM, H_M, HD_M),
            cx[:n_p].reshape(bp, seq, D_BR)[:, seq - (K_A - 1):], ext_a[:, dseq:],
            glu[:n_p].reshape(bp, seq, D_BR)[:, seq - (K_B - 1):], ext_b[:, dseq:],
        ]
        for acc, val in zip(outs, per_layer):
            acc.append(val)

    return (x[:n_p].reshape(bp, seq, d), x[n_p:].reshape(db, dseq, d)) + tuple(jnp.stack(o) for o in outs)
```

```python
import functools

import jax
import jax.numpy as jnp
from jax import lax
from jax.experimental import pallas as pl
from jax.experimental.pallas import tpu as pltpu

F32 = jnp.float32
BF16 = jnp.bfloat16

D_MODEL = 2048
D_BR = D_MODEL // 2
K_A = 3
K_B = 31
HD_C = 128
H_C = D_BR // HD_C
H_M = 4
HD_M = D_BR // H_M
N_MEM = 256
N_BRANCH = 4
PAGE = 128
NORM_EPS = 1e-6
LN_EPS = 1e-5

LANES = 128
SUBLANES = 8
VMEM_LIMIT = 56 * 1024 * 1024

TM = 1040
TM_FFN = 1040
TM_MERGE = 640
TF = 256
DOWN_CHUNK = 512
TN = 512
TN_MERGE = 256
TC = 512
HALO = 32
CONV_ROWS = 32
TQ = 1024
TK = 1024
FOX_ROWS = 256
TQ_M = 512
CS = 256
PAGES_PER_STEP = 16
NEG = -1e30
LOG2E = 1.4426950408889634


def _params(*sem):
    return pltpu.CompilerParams(dimension_semantics=sem, vmem_limit_bytes=VMEM_LIMIT)


def _rms_rows(x, g):
    ms = jnp.mean(x * x, axis=-1, keepdims=True)
    return x * lax.rsqrt(ms + NORM_EPS) * g


def _log_sigmoid(x):
    return jnp.minimum(x, 0.0) - jnp.log1p(jnp.exp(-jnp.abs(x)))


def _dot(a, b):
    return jnp.dot(a, b, preferred_element_type=F32)


def _dot_nt(a, b):
    return lax.dot_general(a, b, (((1,), (1,)), ((), ())), preferred_element_type=F32)


def _ffn_body(emit_h, nf, x_ref, n1_ref, wg_ref, wu_ref, wd_ref, *rest):
    if emit_h:
        n2_ref, o_ref, h2_ref, h_sc = rest
    else:
        o_ref, h_sc = rest
    f = pl.program_id(1)

    @pl.when(f == 0)
    def _():
        x = x_ref[...]
        h_sc[...] = _rms_rows(x, n1_ref[...]).astype(BF16)
        o_ref[...] = x

    h = h_sc[...]
    g = _dot(h, wg_ref[...].astype(BF16))
    u = _dot(h, wu_ref[...].astype(BF16))
    a = (g * jax.nn.sigmoid(g) * u).astype(BF16)
    for c in range(o_ref.shape[1] // DOWN_CHUNK):
        cols = slice(c * DOWN_CHUNK, (c + 1) * DOWN_CHUNK)
        o_ref[:, cols] += 0.5 * _dot(a, wd_ref[:, cols].astype(BF16))

    if emit_h:
        @pl.when(f == nf - 1)
        def _():
            h2_ref[...] = _rms_rows(o_ref[...], n2_ref[...]).astype(BF16)


def _ffn(layer, x, n1, wg, wu, wd, n2=None):
    rows, d = x.shape
    dff = wg.shape[2]
    nf = dff // TF
    emit_h = n2 is not None
    row = lambda i, f: (i, 0)
    fixed = lambda i, f: (0, 0)
    x_spec = pl.BlockSpec((TM_FFN, d), row, pipeline_mode=pl.Buffered(1)) if emit_h else pl.BlockSpec((TM_FFN, d), row)
    in_specs = [x_spec, pl.BlockSpec((1, d), fixed),
                pl.BlockSpec((None, d, TF), lambda i, f: (layer, 0, f)),
                pl.BlockSpec((None, d, TF), lambda i, f: (layer, 0, f)),
                pl.BlockSpec((None, TF, d), lambda i, f: (layer, f, 0))]
    args = [x, n1.reshape(1, d), wg, wu, wd]
    out_shape = [jax.ShapeDtypeStruct((rows, d), F32)]
    out_specs = [pl.BlockSpec((TM_FFN, d), row)]
    if emit_h:
        in_specs.append(pl.BlockSpec((1, d), fixed))
        args.append(n2.reshape(1, d))
        out_shape.append(jax.ShapeDtypeStruct((rows, d), BF16))
        out_specs.append(pl.BlockSpec((TM_FFN, d), row))
    res = pl.pallas_call(
        functools.partial(_ffn_body, emit_h, nf),
        grid=(rows // TM_FFN, nf), in_specs=in_specs, out_specs=out_specs, out_shape=out_shape,
        scratch_shapes=[pltpu.VMEM((TM_FFN, d), BF16)],
        compiler_params=_params("parallel", "arbitrary"), name="ffn_h" if emit_h else "ffn",
    )(*args)
    return res if emit_h else res[0]


def _proj_body(n_w, n_extra, epilogue, h_ref, *refs):
    w_refs = refs[:n_w]
    extra = refs[n_w:n_w + n_extra]
    outs = refs[n_w + n_extra:]
    h = h_ref[...]
    epilogue([_dot(h, w[...]) for w in w_refs], extra, outs)


def _proj(layer, h, w, col_offsets, width, extras, epilogue, out_dtypes, tn=TN, tm=TM, name="proj"):
    rows, d = h.shape
    w_specs = [pl.BlockSpec((None, d, tn), functools.partial(lambda off, i, j: (layer, 0, off + j), off // tn))
               for off in col_offsets]
    e_specs = [pl.BlockSpec(e.shape, lambda i, j: (0, 0)) for e in extras]
    return pl.pallas_call(
        functools.partial(_proj_body, len(col_offsets), len(extras), epilogue),
        grid=(rows // tm, width // tn),
        in_specs=[pl.BlockSpec((tm, d), lambda i, j: (i, 0))] + w_specs + e_specs,
        out_specs=[pl.BlockSpec((tm, tn), lambda i, j: (i, j)) for _ in out_dtypes],
        out_shape=[jax.ShapeDtypeStruct((rows, width), dt) for dt in out_dtypes],
        compiler_params=_params("parallel", "arbitrary"), name=name,
    )(h, *([w] * len(col_offsets)), *extras)


def _head_rms(x, g, hd):
    parts = []
    for c in range(x.shape[1] // hd):
        parts.append(_rms_rows(x[:, c * hd:(c + 1) * hd], g))
    return parts


def _epi_conv_a(accs, extra, outs):
    outs[0][...] = accs[0]
    outs[1][...] = accs[1] * accs[2]


def _epi_glu(accs, extra, outs):
    outs[0][...] = accs[0] * jax.nn.sigmoid(accs[1])


def _epi_fox(accs, extra, outs):
    qn_ref, kn_ref = extra
    q_ref, k_ref, kb_ref, v_ref, vb_ref = outs
    for c, part in enumerate(_head_rms(accs[0], qn_ref[...], HD_C)):
        q_ref[:, c * HD_C:(c + 1) * HD_C] = part.astype(BF16)
    for c, part in enumerate(_head_rms(accs[1], kn_ref[...], HD_C)):
        k_ref[:, c * HD_C:(c + 1) * HD_C] = part
        kb_ref[:, c * HD_C:(c + 1) * HD_C] = part.astype(BF16)
    v_ref[...] = accs[2]
    vb_ref[...] = accs[2].astype(BF16)


def _epi_forget(accs, extra, outs):
    outs[0][...] = _log_sigmoid(accs[0] + extra[0][...])


def _epi_qmem(accs, extra, outs):
    for c, part in enumerate(_head_rms(accs[0], extra[0][...], HD_M)):
        outs[0][:, c * HD_M:(c + 1) * HD_M] = part.astype(BF16)


def _epi_memkv(accs, extra, outs):
    for c, part in enumerate(_head_rms(accs[0], extra[0][...], HD_M)):
        outs[0][:, c * HD_M:(c + 1) * HD_M] = part
    outs[1][...] = accs[1]


def _rms_cast_body(x_ref, g_ref, o_ref):
    o_ref[...] = _rms_rows(x_ref[...], g_ref[...]).astype(BF16)


def _rms_cast(x, g, tm):
    rows, d = x.shape
    return pl.pallas_call(
        _rms_cast_body, grid=(rows // tm,),
        in_specs=[pl.BlockSpec((tm, d), lambda i: (i, 0)), pl.BlockSpec((1, d), lambda i: (0, 0))],
        out_specs=pl.BlockSpec((tm, d), lambda i: (i, 0)),
        out_shape=jax.ShapeDtypeStruct((rows, d), BF16),
        compiler_params=_params("parallel"), name="rms_cast",
    )(x, g.reshape(1, d))


def _ln_silu(x, g, b):
    mu = jnp.mean(x, axis=-1, keepdims=True)
    xc = x - mu
    var = jnp.mean(xc * xc, axis=-1, keepdims=True)
    z = xc * lax.rsqrt(var + LN_EPS) * g + b
    return z * jax.nn.sigmoid(z)


def _conv_prompt_body(tiles_per_seq, ba_ref, cx_ref, cxh_ref, gl_ref, glh_ref, wa_ref, wb_ref,
                      lg_ref, lb_ref, za_ref, zb_ref, sa, sb):
    first = (pl.program_id(0) % tiles_per_seq) == 0
    sa[0:HALO, :] = jnp.where(first, 0.0, cxh_ref[...])
    sa[HALO:, :] = cx_ref[...]
    sb[0, 0:HALO, :] = jnp.where(first, 0.0, glh_ref[...])
    sb[0, HALO:, :] = gl_ref[...]
    acc = None
    for k in range(K_A):
        term = wa_ref[k:k + 1, :] * sa[pl.ds(HALO - (K_A - 1) + k, TC), :]
        acc = term if acc is None else acc + term
    za_ref[...] = (ba_ref[...] * acc).astype(BF16)

    n_shift = TC + HALO - SUBLANES
    for s in range(1, SUBLANES):
        sb[s, 0:n_shift, :] = sb[0, pl.ds(s, n_shift), :]

    def chunk(ci, carry):
        base = pl.multiple_of(ci * CONV_ROWS, CONV_ROWS)
        acc = None
        for k in range(K_B):
            whole, s = divmod(HALO - (K_B - 1) + k, SUBLANES)
            rows = sb[s, pl.ds(base + whole * SUBLANES, CONV_ROWS), :]
            term = wb_ref[k][None] * rows.reshape(CONV_ROWS // SUBLANES, SUBLANES, rows.shape[1])
            acc = term if acc is None else acc + term
        acc = acc.reshape(CONV_ROWS, acc.shape[2])
        zb_ref[pl.ds(base, CONV_ROWS), :] = _ln_silu(acc, lg_ref[...], lb_ref[...]).astype(BF16)
        return carry

    lax.fori_loop(0, TC // CONV_ROWS, chunk, 0)


def _conv_prompt(b_a, cx, glu, n_rows, seq, wa, wb, ln_g, ln_b):
    c = cx.shape[1]
    tile = lambda i: (i, 0)
    halo = lambda i: (jnp.maximum(i * (TC // HALO) - 1, 0), 0)
    fixed = lambda i: (0, 0)
    return pl.pallas_call(
        functools.partial(_conv_prompt_body, seq // TC),
        grid=(n_rows // TC,),
        in_specs=[pl.BlockSpec((TC, c), tile), pl.BlockSpec((TC, c), tile), pl.BlockSpec((HALO, c), halo),
                  pl.BlockSpec((TC, c), tile), pl.BlockSpec((HALO, c), halo),
                  pl.BlockSpec((K_A, c), fixed), pl.BlockSpec((K_B, SUBLANES, c), lambda i: (0, 0, 0)),
                  pl.BlockSpec((1, c), fixed), pl.BlockSpec((1, c), fixed)],
        out_specs=[pl.BlockSpec((TC, c), tile), pl.BlockSpec((TC, c), tile)],
        out_shape=[jax.ShapeDtypeStruct((n_rows, c), BF16)] * 2,
        scratch_shapes=[pltpu.VMEM((TC + HALO, c), F32), pltpu.VMEM((SUBLANES, TC + HALO, c), F32)],
        compiler_params=_params("parallel"), name="conv_prompt",
    )(b_a, cx, cx, glu, glu, wa, jnp.broadcast_to(wb[:, None, :], (K_B, SUBLANES, c)),
      ln_g.reshape(1, c), ln_b.reshape(1, c))


def _conv_sample_body(n_t, ba_ref, ea_ref, eb_ref, wa_ref, wb_ref, lg_ref, lb_ref, za_ref, zb_ref):
    for t in range(n_t):
        acc = None
        for k in range(K_A):
            term = wa_ref[k:k + 1, :] * ea_ref[t + k]
            acc = term if acc is None else acc + term
        za_ref[t] = (ba_ref[t] * acc).astype(BF16)
        acc = None
        for k in range(K_B):
            term = wb_ref[k:k + 1, :] * eb_ref[t + k]
            acc = term if acc is None else acc + term
        zb_ref[t] = _ln_silu(acc, lg_ref[...], lb_ref[...]).astype(BF16)


def _conv_sample(ba_t, ext_a_t, ext_b_t, wa, wb, ln_g, ln_b):
    n_t, nb, c = ba_t.shape
    return pl.pallas_call(
        functools.partial(_conv_sample_body, n_t),
        out_shape=[jax.ShapeDtypeStruct((n_t, nb, c), BF16)] * 2,
        compiler_params=pltpu.CompilerParams(vmem_limit_bytes=VMEM_LIMIT), name="conv_sample",
    )(ba_t, ext_a_t, ext_b_t, wa, wb, ln_g.reshape(1, c), ln_b.reshape(1, c))


def _cumsum_body(x_ref, o_ref, carry):
    @pl.when(pl.program_id(1) == 0)
    def _():
        carry[...] = jnp.zeros_like(carry)

    r = lax.broadcasted_iota(jnp.int32, (CS, CS), 0)
    c = lax.broadcasted_iota(jnp.int32, (CS, CS), 1)
    tri = jnp.where(c <= r, 1.0, 0.0).astype(F32)
    out = jnp.dot(tri, x_ref[...], preferred_element_type=F32, precision=lax.Precision.HIGHEST) + carry[...]
    o_ref[...] = out
    carry[...] = out[CS - 1:CS, :]


def _cumsum_rows(x):
    nb, rows, c = x.shape
    return pl.pallas_call(
        _cumsum_body, grid=(nb, rows // CS),
        in_specs=[pl.BlockSpec((None, CS, c), lambda b, i: (b, i, 0))],
        out_specs=pl.BlockSpec((None, CS, c), lambda b, i: (b, i, 0)),
        out_shape=jax.ShapeDtypeStruct((nb, rows, c), F32),
        scratch_shapes=[pltpu.VMEM((1, c), F32)],
        compiler_params=_params("parallel", "arbitrary"), name="cumsum",
    )(x)


def _fox_prompt_body(qt_ref, kt_ref, q_ref, k_ref, v_ref, cq_ref, ck_ref, o_ref, m_sc, l_sc, acc_sc, cq_sc):
    h = pl.program_id(1)
    i = qt_ref[pl.program_id(2)]
    j = kt_ref[pl.program_id(2)]

    @pl.when(j == 0)
    def _():
        m_sc[...] = jnp.full_like(m_sc, NEG)
        l_sc[...] = jnp.zeros_like(l_sc)
        acc_sc[...] = jnp.zeros_like(acc_sc)
        lane = lax.broadcasted_iota(jnp.int32, cq_ref.shape, 1)
        cq_sc[...] = jnp.sum(jnp.where(lane == h, cq_ref[...], 0.0), axis=1, keepdims=True)

    def update(on_diagonal):
        for rb in range(TQ // FOX_ROWS):
            rows = slice(rb * FOX_ROWS, (rb + 1) * FOX_ROWS)
            n_keys = (rb + 1) * FOX_ROWS if on_diagonal else TK
            t = _dot_nt(q_ref[rows, :], k_ref[0:n_keys, :]) * (HD_C ** -0.5 * LOG2E) - LOG2E * ck_ref[:, 0:n_keys]
            if on_diagonal:
                row = rb * FOX_ROWS + lax.broadcasted_iota(jnp.int32, (FOX_ROWS, n_keys), 0)
                col = lax.broadcasted_iota(jnp.int32, (FOX_ROWS, n_keys), 1)
                t = jnp.where(col <= row, t, -jnp.inf)
            cq = LOG2E * cq_sc[rows, :]
            m_old = m_sc[rows, :]
            m_new = jnp.maximum(m_old, jnp.max(t, axis=1, keepdims=True) + cq)
            alpha = jnp.exp2(m_old - m_new)
            p = jnp.exp2(t - (m_new - cq))
            l_sc[rows, :] = alpha * l_sc[rows, :] + jnp.sum(p, axis=1, keepdims=True)
            acc_sc[rows, :] = alpha * acc_sc[rows, :] + _dot(p.astype(BF16), v_ref[0:n_keys, :])
            m_sc[rows, :] = m_new

    @pl.when(j < i)
    def _():
        update(False)

    @pl.when(j == i)
    def _():
        update(True)
        o_ref[...] = (acc_sc[...] / l_sc[...]).astype(BF16)


def _fox_prompt(qb, kb, vb, c_tok, c_row, nb, seq):
    assert TQ == TK
    nt = seq // TQ
    pairs = [(i, j) for i in range(nt) for j in range(i + 1)]
    q_tiles = jnp.array([p[0] for p in pairs], jnp.int32)
    k_tiles = jnp.array([p[1] for p in pairs], jnp.int32)
    q_map = lambda b, h, p, qt, kt: (b * nt + qt[p], h)
    k_map = lambda b, h, p, qt, kt: (b * nt + kt[p], h)
    grid_spec = pltpu.PrefetchScalarGridSpec(
        num_scalar_prefetch=2, grid=(nb, H_C, len(pairs)),
        in_specs=[pl.BlockSpec((TQ, HD_C), q_map), pl.BlockSpec((TK, HD_C), k_map), pl.BlockSpec((TK, HD_C), k_map),
                  pl.BlockSpec((TQ, LANES), lambda b, h, p, qt, kt: (b * nt + qt[p], 0)),
                  pl.BlockSpec((None, None, 1, TK), lambda b, h, p, qt, kt: (b, h, 0, kt[p]))],
        out_specs=pl.BlockSpec((TQ, HD_C), q_map),
        scratch_shapes=[pltpu.VMEM((TQ, 1), F32), pltpu.VMEM((TQ, 1), F32), pltpu.VMEM((TQ, HD_C), F32),
                        pltpu.VMEM((TQ, 1), F32)])
    return pl.pallas_call(
        _fox_prompt_body, grid_spec=grid_spec,
        out_shape=jax.ShapeDtypeStruct((nb * seq, H_C * HD_C), BF16),
        compiler_params=_params("parallel", "parallel", "arbitrary"), name="fox_prompt",
    )(q_tiles, k_tiles, qb, kb, vb, c_tok, c_row)


def _fox_sample_body(n_steps, pt_ref, q_ref, kn_ref, vn_ref, lfr_ref, lfc_ref, *rest):
    g_pages = PAGES_PER_STEP
    k_refs = rest[:g_pages]
    v_refs = rest[g_pages:2 * g_pages]
    lf_refs = rest[2 * g_pages:3 * g_pages]
    o_ref, m_sc, l_sc, acc_sc, cq_sc, run_sc, lf_sc = rest[3 * g_pages:]
    step = pl.program_id(1)
    n_q = q_ref.shape[0]
    n_key = PAGE * H_C
    scale = HD_C ** -0.5
    q = q_ref[...]

    @pl.when(step == 0)
    def _():
        r = lax.broadcasted_iota(jnp.int32, (n_q, n_q), 0)
        c = lax.broadcasted_iota(jnp.int32, (n_q, n_q), 1)
        ok = ((r % H_C) == (c % H_C)) & (c // H_C <= r // H_C)
        ok_t = ((r % H_C) == (c % H_C)) & (r // H_C <= c // H_C)
        cq_col = jnp.sum(jnp.where(ok, lfr_ref[...], 0.0), axis=1, keepdims=True)
        cq_row = jnp.sum(jnp.where(ok_t, lfc_ref[...], 0.0), axis=0, keepdims=True)
        s = _dot_nt(q, kn_ref[...]) * scale + (cq_col - cq_row)
        s = jnp.where(ok, s, -jnp.inf)
        m = jnp.max(s, axis=1, keepdims=True)
        p = jnp.exp(s - m)
        m_sc[...] = m
        l_sc[...] = jnp.sum(p, axis=1, keepdims=True)
        acc_sc[...] = _dot(p.astype(BF16), vn_ref[...])
        cq_sc[...] = cq_col
        run_sc[...] = jnp.zeros_like(run_sc)

    for g in range(g_pages):
        lf_sc[g:g + 1, :] = lf_refs[g][...]
    lf = lf_sc[...]
    lane = lax.broadcasted_iota(jnp.int32, lf.shape, 1)
    incl = lf
    sh = H_C
    while sh < n_key:
        incl = incl + jnp.where(lane < n_key - sh, pltpu.roll(incl, n_key - sh, axis=1), 0.0)
        sh *= 2
    tot = jnp.where(lane < H_C, incl, 0.0)
    sh = H_C
    while sh < n_key:
        tot = tot + pltpu.roll(tot, sh, axis=1)
        sh *= 2
    after = incl - lf

    r = lax.broadcasted_iota(jnp.int32, (n_q, n_key), 0)
    c = lax.broadcasted_iota(jnp.int32, (n_q, n_key), 1)
    cq_masked = jnp.where((r % H_C) == (c % H_C), cq_sc[...], -jnp.inf)
    run = run_sc[...]
    scores = []
    for g in range(g_pages):
        k2 = k_refs[g][...].reshape(n_key, HD_C).astype(BF16)
        scores.append(_dot_nt(q, k2) * scale + (cq_masked + (after[g:g + 1, :] + run)))
        run = run + tot[g:g + 1, :]
    run_sc[...] = run

    top = scores[0]
    for s in scores[1:]:
        top = jnp.maximum(top, s)
    m_old = m_sc[...]
    m_new = jnp.maximum(m_old, jnp.max(top, axis=1, keepdims=True))
    alpha = jnp.exp(m_old - m_new)
    p_sum = None
    pv = None
    for g in range(g_pages):
        p = jnp.exp(scores[g] - m_new)
        v2 = v_refs[g][...].reshape(n_key, HD_C).astype(BF16)
        term = _dot(p.astype(BF16), v2)
        p_sum = p if p_sum is None else p_sum + p
        pv = term if pv is None else pv + term
    l_new = alpha * l_sc[...] + jnp.sum(p_sum, axis=1, keepdims=True)
    acc_new = alpha * acc_sc[...] + pv
    m_sc[...] = m_new
    l_sc[...] = l_new
    acc_sc[...] = acc_new

    @pl.when(step == n_steps - 1)
    def _():
        o_ref[...] = (acc_new / l_new).astype(BF16)


def _fox_sample(layer, page_table, q, k_new, v_new, lf_row, lf_col, cache_k, cache_v, cache_lf):
    nb, n_q, _ = q.shape
    n_pages = page_table.shape[1]
    g_pages = PAGES_PER_STEP
    n_steps = n_pages // g_pages
    pt = page_table.reshape(-1)

    def page(g, b, s, pt_ref):
        return pt_ref[b * n_pages + (n_pages - 1 - (s * g_pages + g))]

    per_b = lambda b, s, pt_ref: (b, 0, 0)
    kv_specs = [pl.BlockSpec((None, None, PAGE, H_C, HD_C),
                             functools.partial(lambda g, b, s, pt_ref: (layer, page(g, b, s, pt_ref), 0, 0, 0), g))
                for g in range(g_pages)]
    lf_specs = [pl.BlockSpec((None, None, 1, PAGE * H_C),
                             functools.partial(lambda g, b, s, pt_ref: (layer, page(g, b, s, pt_ref), 0, 0), g))
                for g in range(g_pages)]
    grid_spec = pltpu.PrefetchScalarGridSpec(
        num_scalar_prefetch=1, grid=(nb, n_steps),
        in_specs=[pl.BlockSpec((None, n_q, HD_C), per_b), pl.BlockSpec((None, n_q, HD_C), per_b),
                  pl.BlockSpec((None, n_q, HD_C), per_b), pl.BlockSpec((None, 1, n_q), per_b),
                  pl.BlockSpec((None, n_q, 1), per_b)] + kv_specs + kv_specs + lf_specs,
        out_specs=pl.BlockSpec((None, n_q, HD_C), per_b),
        scratch_shapes=[pltpu.VMEM((n_q, 1), F32), pltpu.VMEM((n_q, 1), F32), pltpu.VMEM((n_q, HD_C), F32),
                        pltpu.VMEM((n_q, 1), F32), pltpu.VMEM((1, PAGE * H_C), F32),
                        pltpu.VMEM((g_pages, PAGE * H_C), F32)])
    return pl.pallas_call(
        functools.partial(_fox_sample_body, n_steps), grid_spec=grid_spec,
        out_shape=jax.ShapeDtypeStruct((nb, n_q, HD_C), BF16),
        compiler_params=_params("parallel", "arbitrary"), name="fox_sample",
    )(pt, q, k_new, v_new, lf_row, lf_col, *([cache_k] * g_pages), *([cache_v] * g_pages),
      *([cache_lf] * g_pages))


def _mem_attn_body(q_ref, mk_ref, mv_ref, o_ref):
    q = q_ref[...].astype(BF16)
    for h in range(H_M):
        cols = slice(h * HD_M, (h + 1) * HD_M)
        s = _dot_nt(q[:, cols], mk_ref[:, cols].astype(BF16)) * (HD_M ** -0.5)
        m = jnp.max(s, axis=1, keepdims=True)
        p = jnp.exp(s - m)
        l = jnp.sum(p, axis=1, keepdims=True)
        o = _dot(p.astype(BF16), mv_ref[:, cols].astype(BF16)) / l
        o_ref[:, cols] = o.astype(o_ref.dtype)


def _mem_attn(layer, q, mk, mv, nb, rows_per_b, tq, out_dtype):
    d = q.shape[1]
    nt = rows_per_b // tq
    mem_spec = pl.BlockSpec((None, None, N_MEM, d), lambda b, i: (layer, b, 0, 0))
    return pl.pallas_call(
        _mem_attn_body, grid=(nb, nt),
        in_specs=[pl.BlockSpec((tq, d), lambda b, i: (b * nt + i, 0)), mem_spec, mem_spec],
        out_specs=pl.BlockSpec((tq, d), lambda b, i: (b * nt + i, 0)),
        out_shape=jax.ShapeDtypeStruct((nb * rows_per_b, d), out_dtype),
        compiler_params=_params("parallel", "arbitrary"), name="mem_attn",
    )(q, mk, mv)


def _merge_body(x_ref, h_ref, *refs):
    z_refs = refs[:N_BRANCH]
    wg_refs = refs[N_BRANCH:2 * N_BRANCH]
    w_ref, wo_ref, o_ref = refs[2 * N_BRANCH:]

    @pl.when(pl.program_id(1) == 0)
    def _():
        o_ref[...] = x_ref[...]

    h = h_ref[...]
    merged = None
    for br, (z, wg) in enumerate(zip(z_refs, wg_refs)):
        term = jax.nn.sigmoid(_dot(h, wg[...])) * _dot(z[...], w_ref[br])
        merged = term if merged is None else merged + term
    merged = merged.astype(BF16)
    for c in range(o_ref.shape[1] // DOWN_CHUNK):
        cols = slice(c * DOWN_CHUNK, (c + 1) * DOWN_CHUNK)
        o_ref[:, cols] += _dot(merged, wo_ref[:, cols])


def _merge(layer, x, h, zs, w_all, off_gates, w_outs, w_o):
    rows, d = x.shape
    tn = TN_MERGE
    row = lambda i, n: (i, 0)
    once = pl.Buffered(1)
    in_specs = [pl.BlockSpec((TM_MERGE, d), row, pipeline_mode=once), pl.BlockSpec((TM_MERGE, d), row)]
    in_specs += [pl.BlockSpec((TM_MERGE, D_BR), row, pipeline_mode=once) for _ in range(N_BRANCH)]
    in_specs += [pl.BlockSpec((None, d, tn),
                              functools.partial(lambda blk, i, n: (layer, 0, blk + n), (off_gates + br * d) // tn))
                 for br in range(N_BRANCH)]
    in_specs += [pl.BlockSpec((None, N_BRANCH, D_BR, tn), lambda i, n: (layer, 0, 0, n)),
                 pl.BlockSpec((None, tn, d), lambda i, n: (layer, n, 0))]
    return pl.pallas_call(
        _merge_body, grid=(rows // TM_MERGE, d // tn), in_specs=in_specs,
        out_specs=pl.BlockSpec((TM_MERGE, d), row),
        out_shape=jax.ShapeDtypeStruct((rows, d), F32),
        compiler_params=_params("parallel", "arbitrary"), name="merge",
    )(x, h, *zs, *([w_all] * N_BRANCH), w_outs, w_o)


def kernel(x_prompt, x_sample, cache_fox_k, cache_fox_v, cache_fox_lf, cache_mem_k, cache_mem_v, state_conv_a, state_conv_b, page_table, mem_prompt, n_ffn1, w_ffn1_gate, w_ffn1_up, w_ffn1_down, n_mix, w_in, b_forget, conv_a, w_a_out, conv_b, ln_b_g, ln_b_b, w_b_out, qn_c, kn_c, w_c_out, n_mem, w_mem_k, w_mem_v, qn_m, kn_m, w_m_out, w_o, n_ffn2, w_ffn2_gate, w_ffn2_up, w_ffn2_down):
    depth = w_in.shape[0]
    bp, seq, d = x_prompt.shape
    db, dseq, _ = x_sample.shape
    n_p = bp * seq
    n_s = db * dseq
    n_phys = cache_fox_lf.shape[1]

    x = jnp.concatenate([x_prompt.reshape(n_p, d), x_sample.reshape(n_s, d)], axis=0)
    mem_rows = mem_prompt.reshape(bp * N_MEM, d)
    cache_lf_rows = cache_fox_lf.reshape(depth, n_phys, 1, PAGE * H_C)
    cache_mk = cache_mem_k.reshape(depth, db, N_MEM, D_BR)
    cache_mv = cache_mem_v.reshape(depth, db, N_MEM, D_BR)
    w_branch = jnp.stack([w_a_out, w_b_out, w_c_out, w_m_out], axis=1).astype(BF16)
    w_o_b = w_o.astype(BF16)

    fox_end = 8 * D_BR
    off_ba, off_ca, off_xa, off_g1, off_g2, off_q, off_k, off_v = [s * D_BR for s in range(8)]
    off_qm = fox_end
    off_gates = fox_end + D_BR
    off_f = off_gates + N_BRANCH * D_MODEL

    w_f = jnp.pad(w_in[:, :, fox_end:fox_end + H_C], ((0, 0), (0, 0), (0, LANES - H_C)))
    w_all = jnp.concatenate([w_in[:, :, :fox_end], w_in[:, :, fox_end + H_C:], w_f], axis=2).astype(BF16)
    w_mkv = jnp.concatenate([w_mem_k, w_mem_v], axis=2).astype(BF16)

    outs = [[] for _ in range(12)]
    for l in range(depth):
        b_f = jnp.pad(b_forget[l], (0, LANES - H_C)).reshape(1, LANES)

        x, h = _ffn(l, x, n_ffn1[l], w_ffn1_gate, w_ffn1_up, w_ffn1_down, n_mix[l])

        b_a, cx = _proj(l, h, w_all, [off_ba, off_ca, off_xa], D_BR, [], _epi_conv_a, [F32, F32],
                        name="proj_conv_a")
        (glu,) = _proj(l, h, w_all, [off_g1, off_g2], D_BR, [], _epi_glu, [F32], name="proj_glu")
        q_c, k_c, kb_c, v_c, vb_c = _proj(
            l, h, w_all, [off_q, off_k, off_v], D_BR, [qn_c[l].reshape(1, HD_C), kn_c[l].reshape(1, HD_C)],
            _epi_fox, [BF16, F32, BF16, F32, BF16], name="proj_fox")
        (lf_pad,) = _proj(l, h, w_all, [off_f], LANES, [b_f], _epi_forget, [F32], tn=LANES, name="proj_forget")
        (q_m,) = _proj(l, h, w_all, [off_qm], D_BR, [qn_m[l].reshape(1, HD_M)], _epi_qmem, [BF16],
                       name="proj_qmem")

        m_h = _rms_cast(mem_rows, n_mem[l], N_MEM)
        mk_p, mv_p = _proj(l, m_h, w_mkv, [0, D_BR], D_BR, [kn_m[l].reshape(1, HD_M)], _epi_memkv, [F32, F32],
                           tm=N_MEM, name="proj_memkv")

        za_p, zb_p = _conv_prompt(b_a, cx, glu, n_p, seq, conv_a[l], conv_b[l], ln_b_g[l], ln_b_b[l])
        cx_s = cx[n_p:].reshape(db, dseq, D_BR)
        glu_s = glu[n_p:].reshape(db, dseq, D_BR)
        ext_a = jnp.concatenate([state_conv_a[l], cx_s], axis=1)
        ext_b = jnp.concatenate([state_conv_b[l], glu_s], axis=1)
        ba_t = jnp.swapaxes(b_a[n_p:].reshape(db, dseq, D_BR), 0, 1)
        za_s, zb_s = _conv_sample(ba_t, jnp.swapaxes(ext_a, 0, 1), jnp.swapaxes(ext_b, 0, 1),
                                  conv_a[l], conv_b[l], ln_b_g[l], ln_b_b[l])
        z_a = jnp.concatenate([za_p, jnp.swapaxes(za_s, 0, 1).reshape(n_s, D_BR)], axis=0)
        z_b = jnp.concatenate([zb_p, jnp.swapaxes(zb_s, 0, 1).reshape(n_s, D_BR)], axis=0)

        lf = lf_pad[:, :H_C]
        c_tok = _cumsum_rows(lf_pad[:n_p].reshape(bp, seq, LANES))
        c_row = jnp.swapaxes(c_tok[:, :, :H_C], 1, 2).reshape(bp, H_C, 1, seq)
        yc_p = _fox_prompt(q_c, kb_c, vb_c, c_tok.reshape(n_p, LANES), c_row, bp, seq)
        lf_s = lf[n_p:].reshape(db, dseq * H_C)
        yc_s = _fox_sample(
            l, page_table, q_c[n_p:].reshape(db, dseq * H_C, HD_C), kb_c[n_p:].reshape(db, dseq * H_C, HD_C),
            vb_c[n_p:].reshape(db, dseq * H_C, HD_C), lf_s.reshape(db, 1, dseq * H_C),
            lf_s.reshape(db, dseq * H_C, 1), cache_fox_k, cache_fox_v, cache_lf_rows)
        y_c = jnp.concatenate([yc_p, yc_s.reshape(n_s, D_BR)], axis=0)

        ym_p = _mem_attn(0, q_m, mk_p.reshape(1, bp, N_MEM, D_BR), mv_p.reshape(1, bp, N_MEM, D_BR),
                         bp, seq, TQ_M, BF16)
        qm_s = jnp.pad(q_m[n_p:].reshape(db, dseq, D_BR).astype(F32), ((0, 0), (0, SUBLANES - dseq), (0, 0)))
        ym_s = _mem_attn(l, qm_s.reshape(db * SUBLANES, D_BR), cache_mk, cache_mv, db, SUBLANES, SUBLANES, F32)
        ym_s = ym_s.reshape(db, SUBLANES, D_BR)[:, :dseq].reshape(n_s, D_BR)
        y_m = jnp.concatenate([ym_p, ym_s.astype(BF16)], axis=0)

        x = _merge(l, x, h, [z_a, z_b, y_c, y_m], w_all, off_gates, w_branch, w_o_b)
        x = _ffn(l, x, n_ffn2[l], w_ffn2_gate, w_ffn2_up, w_ffn2_down)

        per_layer = [
            k_c[:n_p].reshape(bp, seq, H_C, HD_C), v_c[:n_p].reshape(bp, seq, H_C, HD_C),
            lf[:n_p].reshape(bp, seq, H_C),
            k_c[n_p:].reshape(db, dseq, H_C, HD_C), v_c[n_p:].reshape(db, dseq, H_C, HD_C),
            lf[n_p:].reshape(db, dseq, H_C),
            mk_p.reshape(bp, N_MEM, H_M, HD_M), mv_p.reshape(bp, N_MEM, H_M, HD_M),
            cx[:n_p].reshape(bp, seq, D_BR)[:, seq - (K_A - 1):], ext_a[:, dseq:],
            glu[:n_p].reshape(bp, seq, D_BR)[:, seq - (K_B - 1):], ext_b[:, dseq:],
        ]
        for acc, val in zip(outs, per_layer):
            acc.append(val)

    return (x[:n_p].reshape(bp, seq, d), x[n_p:].reshape(db, dseq, d)) + tuple(jnp.stack(o) for o in outs)
```

```python
import functools

import jax
import jax.numpy as jnp
from jax import lax
from jax.experimental import pallas as pl
from jax.experimental.pallas import tpu as pltpu

F32 = jnp.float32
BF16 = jnp.bfloat16

D_MODEL = 2048
D_BR = D_MODEL // 2
K_A = 3
K_B = 31
HD_C = 128
H_C = D_BR // HD_C
H_M = 4
HD_M = D_BR // H_M
N_MEM = 256
N_BRANCH = 4
PAGE = 128
NORM_EPS = 1e-6
LN_EPS = 1e-5

LANES = 128
SUBLANES = 8
VMEM_LIMIT = 56 * 1024 * 1024

TM = 1040
TM_FFN = 1040
TM_MERGE = 640
TM_FOX = 512
TF = 256
DOWN_CHUNK = 512
TN = 512
TN_MERGE = 256
TC = 512
HALO = 32
CONV_ROWS = 32
TQ = 1024
TK = 1024
FOX_ROWS = 256
TQ_M = 512
CS = 256
PAGES_PER_STEP = 16
NEG = -1e30
LOG2E = 1.4426950408889634


def _params(*sem):
    return pltpu.CompilerParams(dimension_semantics=sem, vmem_limit_bytes=VMEM_LIMIT)


def _rms_rows(x, g):
    ms = jnp.mean(x * x, axis=-1, keepdims=True)
    return x * lax.rsqrt(ms + NORM_EPS) * g


def _log_sigmoid(x):
    return jnp.minimum(x, 0.0) - jnp.log1p(jnp.exp(-jnp.abs(x)))


def _dot(a, b):
    return jnp.dot(a, b, preferred_element_type=F32)


def _dot_nt(a, b):
    return lax.dot_general(a, b, (((1,), (1,)), ((), ())), preferred_element_type=F32)


def _ffn_body(emit_h, nf, x_ref, n1_ref, wg_ref, wu_ref, wd_ref, *rest):
    if emit_h:
        n2_ref, o_ref, h2_ref, h_sc = rest
    else:
        o_ref, h_sc = rest
    f = pl.program_id(1)

    @pl.when(f == 0)
    def _():
        x = x_ref[...]
        h_sc[...] = _rms_rows(x, n1_ref[...]).astype(BF16)
        o_ref[...] = x

    h = h_sc[...]
    g = _dot(h, wg_ref[...].astype(BF16))
    u = _dot(h, wu_ref[...].astype(BF16))
    a = (g * jax.nn.sigmoid(g) * u).astype(BF16)
    for c in range(o_ref.shape[1] // DOWN_CHUNK):
        cols = slice(c * DOWN_CHUNK, (c + 1) * DOWN_CHUNK)
        o_ref[:, cols] += 0.5 * _dot(a, wd_ref[:, cols].astype(BF16))

    if emit_h:
        @pl.when(f == nf - 1)
        def _():
            h2_ref[...] = _rms_rows(o_ref[...], n2_ref[...]).astype(BF16)


def _ffn(layer, x, n1, wg, wu, wd, n2=None):
    rows, d = x.shape
    dff = wg.shape[2]
    nf = dff // TF
    emit_h = n2 is not None
    row = lambda i, f: (i, 0)
    fixed = lambda i, f: (0, 0)
    x_spec = pl.BlockSpec((TM_FFN, d), row, pipeline_mode=pl.Buffered(1)) if emit_h else pl.BlockSpec((TM_FFN, d), row)
    in_specs = [x_spec, pl.BlockSpec((1, d), fixed),
                pl.BlockSpec((None, d, TF), lambda i, f: (layer, 0, f)),
                pl.BlockSpec((None, d, TF), lambda i, f: (layer, 0, f)),
                pl.BlockSpec((None, TF, d), lambda i, f: (layer, f, 0))]
    args = [x, n1.reshape(1, d), wg, wu, wd]
    out_shape = [jax.ShapeDtypeStruct((rows, d), F32)]
    out_specs = [pl.BlockSpec((TM_FFN, d), row)]
    if emit_h:
        in_specs.append(pl.BlockSpec((1, d), fixed))
        args.append(n2.reshape(1, d))
        out_shape.append(jax.ShapeDtypeStruct((rows, d), BF16))
        out_specs.append(pl.BlockSpec((TM_FFN, d), row))
    res = pl.pallas_call(
        functools.partial(_ffn_body, emit_h, nf),
        grid=(rows // TM_FFN, nf), in_specs=in_specs, out_specs=out_specs, out_shape=out_shape,
        scratch_shapes=[pltpu.VMEM((TM_FFN, d), BF16)],
        compiler_params=_params("parallel", "arbitrary"), name="ffn_h" if emit_h else "ffn",
    )(*args)
    return res if emit_h else res[0]


def _w_prep_body(n_plain, a_ref, b_ref, o_ref):
    j = pl.program_id(1)

    @pl.when(j < n_plain)
    def _():
        o_ref[...] = a_ref[...].astype(BF16)

    @pl.when(j >= n_plain)
    def _():
        x = jnp.concatenate([a_ref[...], b_ref[...]], axis=1)
        x = pltpu.roll(x, x.shape[1] - H_C, axis=1)
        o_ref[...] = x[:, :TN].astype(BF16)


def _w_prep(w_in, n_plain_cols):
    depth, d, n = w_in.shape
    n_out = n - H_C
    return pl.pallas_call(
        functools.partial(_w_prep_body, n_plain_cols // TN), grid=(depth, n_out // TN),
        in_specs=[pl.BlockSpec((None, d, TN), lambda l, j: (l, 0, j)),
                  pl.BlockSpec((None, d, LANES), lambda l, j: (l, 0, (j + 1) * (TN // LANES)))],
        out_specs=pl.BlockSpec((None, d, TN), lambda l, j: (l, 0, j)),
        out_shape=jax.ShapeDtypeStruct((depth, d, n_out), BF16),
        compiler_params=_params("parallel", "arbitrary"), name="w_prep",
    )(w_in, w_in)


def _proj_body(n_w, n_extra, epilogue, h_ref, *refs):
    w_refs = refs[:n_w]
    extra = refs[n_w:n_w + n_extra]
    outs = refs[n_w + n_extra:]
    h = h_ref[...]
    epilogue([_dot(h, w[...].astype(BF16)) for w in w_refs], extra, outs)


def _proj(layer, h, w, col_offsets, width, extras, epilogue, out_dtypes, tn=TN, tm=TM, name="proj",
          row0=0, rows=None):
    d = h.shape[1]
    rows = h.shape[0] if rows is None else rows
    first = row0 // tm
    w_specs = [pl.BlockSpec((None, d, tn), functools.partial(lambda off, i, j: (layer, 0, off + j), off // tn))
               for off in col_offsets]
    e_specs = [pl.BlockSpec(e.shape, lambda i, j: (0, 0)) for e in extras]
    return pl.pallas_call(
        functools.partial(_proj_body, len(col_offsets), len(extras), epilogue),
        grid=(rows // tm, width // tn),
        in_specs=[pl.BlockSpec((tm, d), lambda i, j: (first + i, 0))] + w_specs + e_specs,
        out_specs=[pl.BlockSpec((tm, tn), lambda i, j: (i, j)) for _ in out_dtypes],
        out_shape=[jax.ShapeDtypeStruct((rows, width), dt) for dt in out_dtypes],
        compiler_params=_params("parallel", "arbitrary"), name=name,
    )(h, *([w] * len(col_offsets)), *extras)


def _head_rms(x, g, hd):
    parts = []
    for c in range(x.shape[1] // hd):
        parts.append(_rms_rows(x[:, c * hd:(c + 1) * hd], g))
    return parts


def _epi_conv_a(accs, extra, outs):
    outs[0][...] = accs[0]
    outs[1][...] = accs[1] * accs[2]


def _epi_glu(accs, extra, outs):
    outs[0][...] = accs[0] * jax.nn.sigmoid(accs[1])


def _epi_fox(accs, extra, outs):
    qn_ref, kn_ref = extra
    q_ref, k_ref, kb_ref, v_ref, vb_ref = outs
    for c, part in enumerate(_head_rms(accs[0], qn_ref[...], HD_C)):
        q_ref[:, c * HD_C:(c + 1) * HD_C] = part.astype(BF16)
    for c, part in enumerate(_head_rms(accs[1], kn_ref[...], HD_C)):
        k_ref[:, c * HD_C:(c + 1) * HD_C] = part
        kb_ref[:, c * HD_C:(c + 1) * HD_C] = part.astype(BF16)
    v_ref[...] = accs[2]
    vb_ref[...] = accs[2].astype(BF16)


def _proj_fox_prompt_body(h_ref, wq_ref, wk_ref, wv_ref, qn_ref, kn_ref, q_ref, kb_ref, vb_ref, k4_ref, v4_ref):
    h = h_ref[...]
    acc_v = _dot(h, wv_ref[...])
    for c, part in enumerate(_head_rms(_dot(h, wq_ref[...]), qn_ref[...], HD_C)):
        q_ref[:, c * HD_C:(c + 1) * HD_C] = part.astype(BF16)
    for c, part in enumerate(_head_rms(_dot(h, wk_ref[...]), kn_ref[...], HD_C)):
        k4_ref[:, c, :] = part
        kb_ref[:, c * HD_C:(c + 1) * HD_C] = part.astype(BF16)
    vb_ref[...] = acc_v.astype(BF16)
    for c in range(H_C):
        v4_ref[:, c, :] = acc_v[:, c * HD_C:(c + 1) * HD_C]


def _proj_fox_prompt(layer, h, w, offs, qn, kn, rows):
    d = h.shape[1]
    once = pl.Buffered(1)
    w_specs = [pl.BlockSpec((None, d, D_BR), functools.partial(lambda blk, i: (layer, 0, blk), off // D_BR),
                            pipeline_mode=once) for off in offs]
    fixed = lambda i: (0, 0)
    flat = pl.BlockSpec((TM_FOX, D_BR), lambda i: (i, 0))
    split = pl.BlockSpec((TM_FOX, H_C, HD_C), lambda i: (i, 0, 0))
    return pl.pallas_call(
        _proj_fox_prompt_body, grid=(rows // TM_FOX,),
        in_specs=[pl.BlockSpec((TM_FOX, d), lambda i: (i, 0))] + w_specs
        + [pl.BlockSpec((1, HD_C), fixed), pl.BlockSpec((1, HD_C), fixed)],
        out_specs=[flat, flat, flat, split, split],
        out_shape=[jax.ShapeDtypeStruct((rows, D_BR), BF16)] * 3
        + [jax.ShapeDtypeStruct((rows, H_C, HD_C), F32)] * 2,
        compiler_params=_params("parallel"), name="proj_fox_prompt",
    )(h, w, w, w, qn, kn)


def _epi_forget(accs, extra, outs):
    outs[0][...] = _log_sigmoid(accs[0] + extra[0][...])


def _epi_qmem(accs, extra, outs):
    for c, part in enumerate(_head_rms(accs[0], extra[0][...], HD_M)):
        outs[0][:, c * HD_M:(c + 1) * HD_M] = part.astype(BF16)


def _epi_memkv(accs, extra, outs):
    for c, part in enumerate(_head_rms(accs[0], extra[0][...], HD_M)):
        outs[0][:, c * HD_M:(c + 1) * HD_M] = part
    outs[1][...] = accs[1]


def _rms_cast_body(x_ref, g_ref, o_ref):
    o_ref[...] = _rms_rows(x_ref[...], g_ref[...]).astype(BF16)


def _rms_cast(x, g, tm):
    rows, d = x.shape
    return pl.pallas_call(
        _rms_cast_body, grid=(rows // tm,),
        in_specs=[pl.BlockSpec((tm, d), lambda i: (i, 0)), pl.BlockSpec((1, d), lambda i: (0, 0))],
        out_specs=pl.BlockSpec((tm, d), lambda i: (i, 0)),
        out_shape=jax.ShapeDtypeStruct((rows, d), BF16),
        compiler_params=_params("parallel"), name="rms_cast",
    )(x, g.reshape(1, d))


def _ln_silu(x, g, b):
    mu = jnp.mean(x, axis=-1, keepdims=True)
    xc = x - mu
    var = jnp.mean(xc * xc, axis=-1, keepdims=True)
    z = xc * lax.rsqrt(var + LN_EPS) * g + b
    return z * jax.nn.sigmoid(z)


def _conv_prompt_body(tiles_per_seq, ba_ref, cx_ref, cxh_ref, gl_ref, glh_ref, wa_ref, wb_ref,
                      lg_ref, lb_ref, za_ref, zb_ref, sa, sb):
    first = (pl.program_id(0) % tiles_per_seq) == 0
    sa[0:HALO, :] = jnp.where(first, 0.0, cxh_ref[...])
    sa[HALO:, :] = cx_ref[...]
    sb[0, 0:HALO, :] = jnp.where(first, 0.0, glh_ref[...])
    sb[0, HALO:, :] = gl_ref[...]
    acc = None
    for k in range(K_A):
        term = wa_ref[k:k + 1, :] * sa[pl.ds(HALO - (K_A - 1) + k, TC), :]
        acc = term if acc is None else acc + term
    za_ref[...] = (ba_ref[...] * acc).astype(BF16)

    n_shift = TC + HALO - SUBLANES
    for s in range(1, SUBLANES):
        sb[s, 0:n_shift, :] = sb[0, pl.ds(s, n_shift), :]

    def chunk(ci, carry):
        base = pl.multiple_of(ci * CONV_ROWS, CONV_ROWS)
        acc = None
        for k in range(K_B):
            whole, s = divmod(HALO - (K_B - 1) + k, SUBLANES)
            rows = sb[s, pl.ds(base + whole * SUBLANES, CONV_ROWS), :]
            term = wb_ref[k][None] * rows.reshape(CONV_ROWS // SUBLANES, SUBLANES, rows.shape[1])
            acc = term if acc is None else acc + term
        acc = acc.reshape(CONV_ROWS, acc.shape[2])
        zb_ref[pl.ds(base, CONV_ROWS), :] = _ln_silu(acc, lg_ref[...], lb_ref[...]).astype(BF16)
        return carry

    lax.fori_loop(0, TC // CONV_ROWS, chunk, 0)


def _conv_prompt(b_a, cx, glu, n_rows, seq, wa, wb, ln_g, ln_b):
    c = cx.shape[1]
    tile = lambda i: (i, 0)
    halo = lambda i: (jnp.maximum(i * (TC // HALO) - 1, 0), 0)
    fixed = lambda i: (0, 0)
    return pl.pallas_call(
        functools.partial(_conv_prompt_body, seq // TC),
        grid=(n_rows // TC,),
        in_specs=[pl.BlockSpec((TC, c), tile), pl.BlockSpec((TC, c), tile), pl.BlockSpec((HALO, c), halo),
                  pl.BlockSpec((TC, c), tile), pl.BlockSpec((HALO, c), halo),
                  pl.BlockSpec((K_A, c), fixed), pl.BlockSpec((K_B, SUBLANES, c), lambda i: (0, 0, 0)),
                  pl.BlockSpec((1, c), fixed), pl.BlockSpec((1, c), fixed)],
        out_specs=[pl.BlockSpec((TC, c), tile), pl.BlockSpec((TC, c), tile)],
        out_shape=[jax.ShapeDtypeStruct((n_rows, c), BF16)] * 2,
        scratch_shapes=[pltpu.VMEM((TC + HALO, c), F32), pltpu.VMEM((SUBLANES, TC + HALO, c), F32)],
        compiler_params=_params("parallel"), name="conv_prompt",
    )(b_a, cx, cx, glu, glu, wa, jnp.broadcast_to(wb[:, None, :], (K_B, SUBLANES, c)),
      ln_g.reshape(1, c), ln_b.reshape(1, c))


def _conv_sample_body(n_t, ba_ref, ea_ref, eb_ref, wa_ref, wb_ref, lg_ref, lb_ref, za_ref, zb_ref):
    for t in range(n_t):
        acc = None
        for k in range(K_A):
            term = wa_ref[k:k + 1, :] * ea_ref[t + k]
            acc = term if acc is None else acc + term
        za_ref[t] = (ba_ref[t] * acc).astype(BF16)
        acc = None
        for k in range(K_B):
            term = wb_ref[k:k + 1, :] * eb_ref[t + k]
            acc = term if acc is None else acc + term
        zb_ref[t] = _ln_silu(acc, lg_ref[...], lb_ref[...]).astype(BF16)


def _conv_sample(ba_t, ext_a_t, ext_b_t, wa, wb, ln_g, ln_b):
    n_t, nb, c = ba_t.shape
    return pl.pallas_call(
        functools.partial(_conv_sample_body, n_t),
        out_shape=[jax.ShapeDtypeStruct((n_t, nb, c), BF16)] * 2,
        compiler_params=pltpu.CompilerParams(vmem_limit_bytes=VMEM_LIMIT), name="conv_sample",
    )(ba_t, ext_a_t, ext_b_t, wa, wb, ln_g.reshape(1, c), ln_b.reshape(1, c))


def _cumsum_body(x_ref, o_ref, carry):
    @pl.when(pl.program_id(1) == 0)
    def _():
        carry[...] = jnp.zeros_like(carry)

    r = lax.broadcasted_iota(jnp.int32, (CS, CS), 0)
    c = lax.broadcasted_iota(jnp.int32, (CS, CS), 1)
    tri = jnp.where(c <= r, 1.0, 0.0).astype(F32)
    out = jnp.dot(tri, x_ref[...], preferred_element_type=F32, precision=lax.Precision.HIGHEST) + carry[...]
    o_ref[...] = out
    carry[...] = out[CS - 1:CS, :]


def _cumsum_rows(x):
    nb, rows, c = x.shape
    return pl.pallas_call(
        _cumsum_body, grid=(nb, rows // CS),
        in_specs=[pl.BlockSpec((None, CS, c), lambda b, i: (b, i, 0))],
        out_specs=pl.BlockSpec((None, CS, c), lambda b, i: (b, i, 0)),
        out_shape=jax.ShapeDtypeStruct((nb, rows, c), F32),
        scratch_shapes=[pltpu.VMEM((1, c), F32)],
        compiler_params=_params("parallel", "arbitrary"), name="cumsum",
    )(x)


def _fox_prompt_body(qt_ref, kt_ref, q_ref, k_ref, v_ref, cq_ref, ck_ref, o_ref, m_sc, l_sc, acc_sc, cq_sc):
    h = pl.program_id(1)
    i = qt_ref[pl.program_id(2)]
    j = kt_ref[pl.program_id(2)]

    @pl.when(j == 0)
    def _():
        m_sc[...] = jnp.full_like(m_sc, NEG)
        l_sc[...] = jnp.zeros_like(l_sc)
        acc_sc[...] = jnp.zeros_like(acc_sc)
        lane = lax.broadcasted_iota(jnp.int32, cq_ref.shape, 1)
        cq_sc[...] = jnp.sum(jnp.where(lane == h, cq_ref[...], 0.0), axis=1, keepdims=True)

    def update(on_diagonal):
        for rb in range(TQ // FOX_ROWS):
            rows = slice(rb * FOX_ROWS, (rb + 1) * FOX_ROWS)
            n_keys = (rb + 1) * FOX_ROWS if on_diagonal else TK
            t = _dot_nt(q_ref[rows, :], k_ref[0:n_keys, :]) * (HD_C ** -0.5 * LOG2E) - LOG2E * ck_ref[:, 0:n_keys]
            if on_diagonal:
                row = rb * FOX_ROWS + lax.broadcasted_iota(jnp.int32, (FOX_ROWS, n_keys), 0)
                col = lax.broadcasted_iota(jnp.int32, (FOX_ROWS, n_keys), 1)
                t = jnp.where(col <= row, t, -jnp.inf)
            cq = LOG2E * cq_sc[rows, :]
            m_old = m_sc[rows, :]
            m_new = jnp.maximum(m_old, jnp.max(t, axis=1, keepdims=True) + cq)
            alpha = jnp.exp2(m_old - m_new)
            p = jnp.exp2(t - (m_new - cq))
            l_sc[rows, :] = alpha * l_sc[rows, :] + jnp.sum(p, axis=1, keepdims=True)
            acc_sc[rows, :] = alpha * acc_sc[rows, :] + _dot(p.astype(BF16), v_ref[0:n_keys, :])
            m_sc[rows, :] = m_new

    @pl.when(j < i)
    def _():
        update(False)

    @pl.when(j == i)
    def _():
        update(True)
        o_ref[...] = (acc_sc[...] / l_sc[...]).astype(BF16)


def _fox_prompt(qb, kb, vb, c_tok, c_row, nb, seq):
    assert TQ == TK
    nt = seq // TQ
    pairs = [(i, j) for i in range(nt) for j in range(i + 1)]
    q_tiles = jnp.array([p[0] for p in pairs], jnp.int32)
    k_tiles = jnp.array([p[1] for p in pairs], jnp.int32)
    q_map = lambda b, h, p, qt, kt: (b * nt + qt[p], h)
    k_map = lambda b, h, p, qt, kt: (b * nt + kt[p], h)
    grid_spec = pltpu.PrefetchScalarGridSpec(
        num_scalar_prefetch=2, grid=(nb, H_C, len(pairs)),
        in_specs=[pl.BlockSpec((TQ, HD_C), q_map), pl.BlockSpec((TK, HD_C), k_map), pl.BlockSpec((TK, HD_C), k_map),
                  pl.BlockSpec((TQ, LANES), lambda b, h, p, qt, kt: (b * nt + qt[p], 0)),
                  pl.BlockSpec((None, None, 1, TK), lambda b, h, p, qt, kt: (b, h, 0, kt[p]))],
        out_specs=pl.BlockSpec((TQ, HD_C), q_map),
        scratch_shapes=[pltpu.VMEM((TQ, 1), F32), pltpu.VMEM((TQ, 1), F32), pltpu.VMEM((TQ, HD_C), F32),
                        pltpu.VMEM((TQ, 1), F32)])
    return pl.pallas_call(
        _fox_prompt_body, grid_spec=grid_spec,
        out_shape=jax.ShapeDtypeStruct((nb * seq, H_C * HD_C), BF16),
        compiler_params=_params("parallel", "parallel", "arbitrary"), name="fox_prompt",
    )(q_tiles, k_tiles, qb, kb, vb, c_tok, c_row)


def _fox_sample_body(n_steps, pt_ref, q_ref, kn_ref, vn_ref, lfr_ref, lfc_ref, *rest):
    g_pages = PAGES_PER_STEP
    k_refs = rest[:g_pages]
    v_refs = rest[g_pages:2 * g_pages]
    lf_refs = rest[2 * g_pages:3 * g_pages]
    o_ref, m_sc, l_sc, acc_sc, cq_sc, run_sc, lf_sc = rest[3 * g_pages:]
    step = pl.program_id(1)
    n_q = q_ref.shape[0]
    n_key = PAGE * H_C
    scale = HD_C ** -0.5
    q = q_ref[...]

    @pl.when(step == 0)
    def _():
        r = lax.broadcasted_iota(jnp.int32, (n_q, n_q), 0)
        c = lax.broadcasted_iota(jnp.int32, (n_q, n_q), 1)
        ok = ((r % H_C) == (c % H_C)) & (c // H_C <= r // H_C)
        ok_t = ((r % H_C) == (c % H_C)) & (r // H_C <= c // H_C)
        cq_col = jnp.sum(jnp.where(ok, lfr_ref[...], 0.0), axis=1, keepdims=True)
        cq_row = jnp.sum(jnp.where(ok_t, lfc_ref[...], 0.0), axis=0, keepdims=True)
        s = _dot_nt(q, kn_ref[...]) * scale + (cq_col - cq_row)
        s = jnp.where(ok, s, -jnp.inf)
        m = jnp.max(s, axis=1, keepdims=True)
        p = jnp.exp(s - m)
        m_sc[...] = m
        l_sc[...] = jnp.sum(p, axis=1, keepdims=True)
        acc_sc[...] = _dot(p.astype(BF16), vn_ref[...])
        cq_sc[...] = cq_col
        run_sc[...] = jnp.zeros_like(run_sc)

    for g in range(g_pages):
        lf_sc[g:g + 1, :] = lf_refs[g][...]
    lf = lf_sc[...]
    lane = lax.broadcasted_iota(jnp.int32, lf.shape, 1)
    incl = lf
    sh = H_C
    while sh < n_key:
        incl = incl + jnp.where(lane < n_key - sh, pltpu.roll(incl, n_key - sh, axis=1), 0.0)
        sh *= 2
    tot = jnp.where(lane < H_C, incl, 0.0)
    sh = H_C
    while sh < n_key:
        tot = tot + pltpu.roll(tot, sh, axis=1)
        sh *= 2
    after = incl - lf

    r = lax.broadcasted_iota(jnp.int32, (n_q, n_key), 0)
    c = lax.broadcasted_iota(jnp.int32, (n_q, n_key), 1)
    cq_masked = jnp.where((r % H_C) == (c % H_C), cq_sc[...], -jnp.inf)
    run = run_sc[...]
    scores = []
    for g in range(g_pages):
        k2 = k_refs[g][...].reshape(n_key, HD_C).astype(BF16)
        scores.append(_dot_nt(q, k2) * scale + (cq_masked + (after[g:g + 1, :] + run)))
        run = run + tot[g:g + 1, :]
    run_sc[...] = run

    top = scores[0]
    for s in scores[1:]:
        top = jnp.maximum(top, s)
    m_old = m_sc[...]
    m_new = jnp.maximum(m_old, jnp.max(top, axis=1, keepdims=True))
    alpha = jnp.exp(m_old - m_new)
    p_sum = None
    pv = None
    for g in range(g_pages):
        p = jnp.exp(scores[g] - m_new)
        v2 = v_refs[g][...].reshape(n_key, HD_C).astype(BF16)
        term = _dot(p.astype(BF16), v2)
        p_sum = p if p_sum is None else p_sum + p
        pv = term if pv is None else pv + term
    l_new = alpha * l_sc[...] + jnp.sum(p_sum, axis=1, keepdims=True)
    acc_new = alpha * acc_sc[...] + pv
    m_sc[...] = m_new
    l_sc[...] = l_new
    acc_sc[...] = acc_new

    @pl.when(step == n_steps - 1)
    def _():
        o_ref[...] = (acc_new / l_new).astype(BF16)


def _fox_sample(layer, page_table, q, k_new, v_new, lf_row, lf_col, cache_k, cache_v, cache_lf):
    nb, n_q, _ = q.shape
    n_pages = page_table.shape[1]
    g_pages = PAGES_PER_STEP
    n_steps = n_pages // g_pages
    pt = page_table.reshape(-1)

    def page(g, b, s, pt_ref):
        return pt_ref[b * n_pages + (n_pages - 1 - (s * g_pages + g))]

    per_b = lambda b, s, pt_ref: (b, 0, 0)
    kv_specs = [pl.BlockSpec((None, None, PAGE, H_C, HD_C),
                             functools.partial(lambda g, b, s, pt_ref: (layer, page(g, b, s, pt_ref), 0, 0, 0), g))
                for g in range(g_pages)]
    lf_specs = [pl.BlockSpec((None, None, 1, PAGE * H_C),
                             functools.partial(lambda g, b, s, pt_ref: (layer, page(g, b, s, pt_ref), 0, 0), g))
                for g in range(g_pages)]
    grid_spec = pltpu.PrefetchScalarGridSpec(
        num_scalar_prefetch=1, grid=(nb, n_steps),
        in_specs=[pl.BlockSpec((None, n_q, HD_C), per_b), pl.BlockSpec((None, n_q, HD_C), per_b),
                  pl.BlockSpec((None, n_q, HD_C), per_b), pl.BlockSpec((None, 1, n_q), per_b),
                  pl.BlockSpec((None, n_q, 1), per_b)] + kv_specs + kv_specs + lf_specs,
        out_specs=pl.BlockSpec((None, n_q, HD_C), per_b),
        scratch_shapes=[pltpu.VMEM((n_q, 1), F32), pltpu.VMEM((n_q, 1), F32), pltpu.VMEM((n_q, HD_C), F32),
                        pltpu.VMEM((n_q, 1), F32), pltpu.VMEM((1, PAGE * H_C), F32),
                        pltpu.VMEM((g_pages, PAGE * H_C), F32)])
    return pl.pallas_call(
        functools.partial(_fox_sample_body, n_steps), grid_spec=grid_spec,
        out_shape=jax.ShapeDtypeStruct((nb, n_q, HD_C), BF16),
        compiler_params=_params("parallel", "arbitrary"), name="fox_sample",
    )(pt, q, k_new, v_new, lf_row, lf_col, *([cache_k] * g_pages), *([cache_v] * g_pages),
      *([cache_lf] * g_pages))


def _mem_attn_body(q_ref, mk_ref, mv_ref, o_ref):
    q = q_ref[...].astype(BF16)
    for h in range(H_M):
        cols = slice(h * HD_M, (h + 1) * HD_M)
        s = _dot_nt(q[:, cols], mk_ref[:, cols].astype(BF16)) * (HD_M ** -0.5)
        m = jnp.max(s, axis=1, keepdims=True)
        p = jnp.exp(s - m)
        l = jnp.sum(p, axis=1, keepdims=True)
        o = _dot(p.astype(BF16), mv_ref[:, cols].astype(BF16)) / l
        o_ref[:, cols] = o.astype(o_ref.dtype)


def _mem_attn(layer, q, mk, mv, nb, rows_per_b, tq, out_dtype):
    d = q.shape[1]
    nt = rows_per_b // tq
    mem_spec = pl.BlockSpec((None, None, N_MEM, d), lambda b, i: (layer, b, 0, 0))
    return pl.pallas_call(
        _mem_attn_body, grid=(nb, nt),
        in_specs=[pl.BlockSpec((tq, d), lambda b, i: (b * nt + i, 0)), mem_spec, mem_spec],
        out_specs=pl.BlockSpec((tq, d), lambda b, i: (b * nt + i, 0)),
        out_shape=jax.ShapeDtypeStruct((nb * rows_per_b, d), out_dtype),
        compiler_params=_params("parallel", "arbitrary"), name="mem_attn",
    )(q, mk, mv)


def _merge_body(x_ref, h_ref, *refs):
    z_refs = refs[:N_BRANCH]
    wg_refs = refs[N_BRANCH:2 * N_BRANCH]
    w_ref, wo_ref, o_ref = refs[2 * N_BRANCH:]

    @pl.when(pl.program_id(1) == 0)
    def _():
        o_ref[...] = x_ref[...]

    h = h_ref[...]
    merged = None
    for br, (z, wg) in enumerate(zip(z_refs, wg_refs)):
        term = jax.nn.sigmoid(_dot(h, wg[...])) * _dot(z[...], w_ref[br])
        merged = term if merged is None else merged + term
    merged = merged.astype(BF16)
    for c in range(o_ref.shape[1] // DOWN_CHUNK):
        cols = slice(c * DOWN_CHUNK, (c + 1) * DOWN_CHUNK)
        o_ref[:, cols] += _dot(merged, wo_ref[:, cols])


def _merge(layer, x, h, zs, w_all, off_gates, w_outs, w_o):
    rows, d = x.shape
    tn = TN_MERGE
    row = lambda i, n: (i, 0)
    once = pl.Buffered(1)
    in_specs = [pl.BlockSpec((TM_MERGE, d), row, pipeline_mode=once), pl.BlockSpec((TM_MERGE, d), row)]
    in_specs += [pl.BlockSpec((TM_MERGE, D_BR), row, pipeline_mode=once) for _ in range(N_BRANCH)]
    in_specs += [pl.BlockSpec((None, d, tn),
                              functools.partial(lambda blk, i, n: (layer, 0, blk + n), (off_gates + br * d) // tn))
                 for br in range(N_BRANCH)]
    in_specs += [pl.BlockSpec((None, N_BRANCH, D_BR, tn), lambda i, n: (layer, 0, 0, n)),
                 pl.BlockSpec((None, tn, d), lambda i, n: (layer, n, 0))]
    return pl.pallas_call(
        _merge_body, grid=(rows // TM_MERGE, d // tn), in_specs=in_specs,
        out_specs=pl.BlockSpec((TM_MERGE, d), row),
        out_shape=jax.ShapeDtypeStruct((rows, d), F32),
        compiler_params=_params("parallel", "arbitrary"), name="merge",
    )(x, h, *zs, *([w_all] * N_BRANCH), w_outs, w_o)


def kernel(x_prompt, x_sample, cache_fox_k, cache_fox_v, cache_fox_lf, cache_mem_k, cache_mem_v, state_conv_a, state_conv_b, page_table, mem_prompt, n_ffn1, w_ffn1_gate, w_ffn1_up, w_ffn1_down, n_mix, w_in, b_forget, conv_a, w_a_out, conv_b, ln_b_g, ln_b_b, w_b_out, qn_c, kn_c, w_c_out, n_mem, w_mem_k, w_mem_v, qn_m, kn_m, w_m_out, w_o, n_ffn2, w_ffn2_gate, w_ffn2_up, w_ffn2_down):
    depth = w_in.shape[0]
    bp, seq, d = x_prompt.shape
    db, dseq, _ = x_sample.shape
    n_p = bp * seq
    n_s = db * dseq
    n_phys = cache_fox_lf.shape[1]

    x = jnp.concatenate([x_prompt.reshape(n_p, d), x_sample.reshape(n_s, d)], axis=0)
    mem_rows = mem_prompt.reshape(bp * N_MEM, d)
    cache_lf_rows = cache_fox_lf.reshape(depth, n_phys, 1, PAGE * H_C)
    cache_mk = cache_mem_k.reshape(depth, db, N_MEM, D_BR)
    cache_mv = cache_mem_v.reshape(depth, db, N_MEM, D_BR)
    w_branch = jnp.stack([w_a_out, w_b_out, w_c_out, w_m_out], axis=1).astype(BF16)
    w_o_b = w_o.astype(BF16)

    fox_end = 8 * D_BR
    off_ba, off_ca, off_xa, off_g1, off_g2, off_q, off_k, off_v = [s * D_BR for s in range(8)]
    off_qm = fox_end
    off_gates = fox_end + D_BR

    w_all = _w_prep(w_in, fox_end)
    w_mkv = jnp.concatenate([w_mem_k, w_mem_v], axis=2).astype(BF16)

    outs = [[] for _ in range(12)]
    for l in range(depth):
        b_f = jnp.pad(b_forget[l], (0, LANES - H_C)).reshape(1, LANES)

        x, h = _ffn(l, x, n_ffn1[l], w_ffn1_gate, w_ffn1_up, w_ffn1_down, n_mix[l])

        b_a, cx = _proj(l, h, w_all, [off_ba, off_ca, off_xa], D_BR, [], _epi_conv_a, [F32, F32],
                        name="proj_conv_a")
        (glu,) = _proj(l, h, w_all, [off_g1, off_g2], D_BR, [], _epi_glu, [F32], name="proj_glu")
        fox_norms = [qn_c[l].reshape(1, HD_C), kn_c[l].reshape(1, HD_C)]
        q_c, kb_c, vb_c, k_p, v_p = _proj_fox_prompt(l, h, w_all, [off_q, off_k, off_v], *fox_norms, n_p)
        q_s, k_s, kb_s, v_s, vb_s = _proj(
            l, h, w_all, [off_q, off_k, off_v], D_BR, fox_norms, _epi_fox, [BF16, F32, BF16, F32, BF16],
            tm=n_s, name="proj_fox_sample", row0=n_p, rows=n_s)
        (lf_pad,) = _proj(l, h, w_in, [fox_end], LANES, [b_f], _epi_forget, [F32], tn=LANES, name="proj_forget")
        (q_m,) = _proj(l, h, w_all, [off_qm], D_BR, [qn_m[l].reshape(1, HD_M)], _epi_qmem, [BF16],
                       name="proj_qmem")

        m_h = _rms_cast(mem_rows, n_mem[l], N_MEM)
        mk_p, mv_p = _proj(l, m_h, w_mkv, [0, D_BR], D_BR, [kn_m[l].reshape(1, HD_M)], _epi_memkv, [F32, F32],
                           tm=N_MEM, name="proj_memkv")

        za_p, zb_p = _conv_prompt(b_a, cx, glu, n_p, seq, conv_a[l], conv_b[l], ln_b_g[l], ln_b_b[l])
        cx_s = cx[n_p:].reshape(db, dseq, D_BR)
        glu_s = glu[n_p:].reshape(db, dseq, D_BR)
        ext_a = jnp.concatenate([state_conv_a[l], cx_s], axis=1)
        ext_b = jnp.concatenate([state_conv_b[l], glu_s], axis=1)
        ba_t = jnp.swapaxes(b_a[n_p:].reshape(db, dseq, D_BR), 0, 1)
        za_s, zb_s = _conv_sample(ba_t, jnp.swapaxes(ext_a, 0, 1), jnp.swapaxes(ext_b, 0, 1),
                                  conv_a[l], conv_b[l], ln_b_g[l], ln_b_b[l])
        z_a = jnp.concatenate([za_p, jnp.swapaxes(za_s, 0, 1).reshape(n_s, D_BR)], axis=0)
        z_b = jnp.concatenate([zb_p, jnp.swapaxes(zb_s, 0, 1).reshape(n_s, D_BR)], axis=0)

        lf = lf_pad[:, :H_C]
        c_tok = _cumsum_rows(lf_pad[:n_p].reshape(bp, seq, LANES))
        c_row = jnp.swapaxes(c_tok[:, :, :H_C], 1, 2).reshape(bp, H_C, 1, seq)
        yc_p = _fox_prompt(q_c, kb_c, vb_c, c_tok.reshape(n_p, LANES), c_row, bp, seq)
        lf_s = lf[n_p:].reshape(db, dseq * H_C)
        yc_s = _fox_sample(
            l, page_table, q_s.reshape(db, dseq * H_C, HD_C), kb_s.reshape(db, dseq * H_C, HD_C),
            vb_s.reshape(db, dseq * H_C, HD_C), lf_s.reshape(db, 1, dseq * H_C),
            lf_s.reshape(db, dseq * H_C, 1), cache_fox_k, cache_fox_v, cache_lf_rows)
        y_c = jnp.concatenate([yc_p, yc_s.reshape(n_s, D_BR)], axis=0)

        ym_p = _mem_attn(0, q_m, mk_p.reshape(1, bp, N_MEM, D_BR), mv_p.reshape(1, bp, N_MEM, D_BR),
                         bp, seq, TQ_M, BF16)
        qm_s = jnp.pad(q_m[n_p:].reshape(db, dseq, D_BR).astype(F32), ((0, 0), (0, SUBLANES - dseq), (0, 0)))
        ym_s = _mem_attn(l, qm_s.reshape(db * SUBLANES, D_BR), cache_mk, cache_mv, db, SUBLANES, SUBLANES, F32)
        ym_s = ym_s.reshape(db, SUBLANES, D_BR)[:, :dseq].reshape(n_s, D_BR)
        y_m = jnp.concatenate([ym_p, ym_s.astype(BF16)], axis=0)

        x = _merge(l, x, h, [z_a, z_b, y_c, y_m], w_all, off_gates, w_branch, w_o_b)
        x = _ffn(l, x, n_ffn2[l], w_ffn2_gate, w_ffn2_up, w_ffn2_down)

        per_layer = [
            k_p.reshape(bp, seq, H_C, HD_C), v_p.reshape(bp, seq, H_C, HD_C),
            lf[:n_p].reshape(bp, seq, H_C),
            k_s.reshape(db, dseq, H_C, HD_C), v_s.reshape(db, dseq, H_C, HD_C),
            lf[n_p:].reshape(db, dseq, H_C),
            mk_p.reshape(bp, N_MEM, H_M, HD_M), mv_p.reshape(bp, N_MEM, H_M, HD_M),
            cx[:n_p].reshape(bp, seq, D_BR)[:, seq - (K_A - 1):], ext_a[:, dseq:],
            glu[:n_p].reshape(bp, seq, D_BR)[:, seq - (K_B - 1):], ext_b[:, dseq:],
        ]
        for acc, val in zip(outs, per_layer):
            acc.append(val)

    return (x[:n_p].reshape(bp, seq, d), x[n_p:].reshape(db, dseq, d)) + tuple(jnp.stack(o) for o in outs)
```

```python
import functools

import jax
import jax.numpy as jnp
from jax import lax
from jax.experimental import pallas as pl
from jax.experimental.pallas import tpu as pltpu

F32 = jnp.float32
BF16 = jnp.bfloat16

D_MODEL = 2048
D_BR = D_MODEL // 2
K_A = 3
K_B = 31
HD_C = 128
H_C = D_BR // HD_C
H_M = 4
HD_M = D_BR // H_M
N_MEM = 256
N_BRANCH = 4
PAGE = 128
NORM_EPS = 1e-6
LN_EPS = 1e-5

LANES = 128
SUBLANES = 8
VMEM_LIMIT = 56 * 1024 * 1024

TM = 1040
TM_FFN = 1040
TM_MERGE = 640
TM_FOX = 512
TF = 256
DOWN_CHUNK = 512
TN = 512
TN_MERGE = 256
TC = 512
HALO = 32
CONV_ROWS = 32
TQ = 1024
TK = 1024
FOX_ROWS = 256
TQ_M = 512
CS = 256
PAGES_PER_STEP = 16
NEG = -1e30
LOG2E = 1.4426950408889634


def _params(*sem):
    return pltpu.CompilerParams(dimension_semantics=sem, vmem_limit_bytes=VMEM_LIMIT)


def _rms_rows(x, g):
    ms = jnp.mean(x * x, axis=-1, keepdims=True)
    return x * lax.rsqrt(ms + NORM_EPS) * g


def _log_sigmoid(x):
    return jnp.minimum(x, 0.0) - jnp.log1p(jnp.exp(-jnp.abs(x)))


def _dot(a, b):
    return jnp.dot(a, b, preferred_element_type=F32)


def _dot_nt(a, b):
    return lax.dot_general(a, b, (((1,), (1,)), ((), ())), preferred_element_type=F32)


def _ffn_body(emit_h, nf, x_ref, n1_ref, wg_ref, wu_ref, wd_ref, *rest):
    if emit_h:
        n2_ref, o_ref, h2_ref, h_sc = rest
    else:
        o_ref, h_sc = rest
    f = pl.program_id(1)

    @pl.when(f == 0)
    def _():
        x = x_ref[...]
        h_sc[...] = _rms_rows(x, n1_ref[...]).astype(BF16)
        o_ref[...] = x

    h = h_sc[...]
    g = _dot(h, wg_ref[...].astype(BF16))
    u = _dot(h, wu_ref[...].astype(BF16))
    a = (g * jax.nn.sigmoid(g) * u).astype(BF16)
    for c in range(o_ref.shape[1] // DOWN_CHUNK):
        cols = slice(c * DOWN_CHUNK, (c + 1) * DOWN_CHUNK)
        o_ref[:, cols] += 0.5 * _dot(a, wd_ref[:, cols].astype(BF16))

    if emit_h:
        @pl.when(f == nf - 1)
        def _():
            h2_ref[...] = _rms_rows(o_ref[...], n2_ref[...]).astype(BF16)


def _ffn(layer, x, n1, wg, wu, wd, n2=None):
    rows, d = x.shape
    dff = wg.shape[2]
    nf = dff // TF
    emit_h = n2 is not None
    row = lambda i, f: (i, 0)
    fixed = lambda i, f: (0, 0)
    x_spec = pl.BlockSpec((TM_FFN, d), row, pipeline_mode=pl.Buffered(1)) if emit_h else pl.BlockSpec((TM_FFN, d), row)
    in_specs = [x_spec, pl.BlockSpec((1, d), fixed),
                pl.BlockSpec((None, d, TF), lambda i, f: (layer, 0, f)),
                pl.BlockSpec((None, d, TF), lambda i, f: (layer, 0, f)),
                pl.BlockSpec((None, TF, d), lambda i, f: (layer, f, 0))]
    args = [x, n1.reshape(1, d), wg, wu, wd]
    out_shape = [jax.ShapeDtypeStruct((rows, d), F32)]
    out_specs = [pl.BlockSpec((TM_FFN, d), row)]
    if emit_h:
        in_specs.append(pl.BlockSpec((1, d), fixed))
        args.append(n2.reshape(1, d))
        out_shape.append(jax.ShapeDtypeStruct((rows, d), BF16))
        out_specs.append(pl.BlockSpec((TM_FFN, d), row))
    res = pl.pallas_call(
        functools.partial(_ffn_body, emit_h, nf),
        grid=(rows // TM_FFN, nf), in_specs=in_specs, out_specs=out_specs, out_shape=out_shape,
        scratch_shapes=[pltpu.VMEM((TM_FFN, d), BF16)],
        compiler_params=_params("parallel", "arbitrary"), name="ffn_h" if emit_h else "ffn",
    )(*args)
    return res if emit_h else res[0]


def _w_prep_body(n_plain, a_ref, b_ref, o_ref):
    j = pl.program_id(1)

    @pl.when(j < n_plain)
    def _():
        o_ref[...] = a_ref[...].T.astype(BF16)

    @pl.when(j >= n_plain)
    def _():
        x = jnp.concatenate([a_ref[H_C:, :], b_ref[...]], axis=0)
        o_ref[...] = x.T.astype(BF16)


def _w_prep(w_t, n_plain_cols):
    depth, n, d = w_t.shape
    n_out = n - H_C
    return pl.pallas_call(
        functools.partial(_w_prep_body, n_plain_cols // TN), grid=(depth, n_out // TN),
        in_specs=[pl.BlockSpec((None, TN, d), lambda l, j: (l, j, 0)),
                  pl.BlockSpec((None, H_C, d), lambda l, j: (l, (j + 1) * (TN // H_C), 0))],
        out_specs=pl.BlockSpec((None, d, TN), lambda l, j: (l, 0, j)),
        out_shape=jax.ShapeDtypeStruct((depth, d, n_out), BF16),
        compiler_params=_params("parallel", "arbitrary"), name="w_prep",
    )(w_t, w_t)


def _proj_body(n_w, n_extra, epilogue, transposed, h_ref, *refs):
    w_refs = refs[:n_w]
    extra = refs[n_w:n_w + n_extra]
    outs = refs[n_w + n_extra:]
    h = h_ref[...]
    dot = _dot_nt if transposed else _dot
    epilogue([dot(h, w[...].astype(BF16)) for w in w_refs], extra, outs)


def _proj(layer, h, w, col_offsets, width, extras, epilogue, out_dtypes, tn=TN, tm=TM, name="proj",
          row0=0, rows=None, transposed=False):
    d = h.shape[1]
    rows = h.shape[0] if rows is None else rows
    first = row0 // tm
    if transposed:
        w_specs = [pl.BlockSpec((None, tn, d), functools.partial(lambda off, i, j: (layer, off + j, 0), off // tn))
                   for off in col_offsets]
    else:
        w_specs = [pl.BlockSpec((None, d, tn), functools.partial(lambda off, i, j: (layer, 0, off + j), off // tn))
                   for off in col_offsets]
    e_specs = [pl.BlockSpec(e.shape, lambda i, j: (0, 0)) for e in extras]
    return pl.pallas_call(
        functools.partial(_proj_body, len(col_offsets), len(extras), epilogue, transposed),
        grid=(rows // tm, width // tn),
        in_specs=[pl.BlockSpec((tm, d), lambda i, j: (first + i, 0))] + w_specs + e_specs,
        out_specs=[pl.BlockSpec((tm, tn), lambda i, j: (i, j)) for _ in out_dtypes],
        out_shape=[jax.ShapeDtypeStruct((rows, width), dt) for dt in out_dtypes],
        compiler_params=_params("parallel", "arbitrary"), name=name,
    )(h, *([w] * len(col_offsets)), *extras)


def _head_rms(x, g, hd):
    parts = []
    for c in range(x.shape[1] // hd):
        parts.append(_rms_rows(x[:, c * hd:(c + 1) * hd], g))
    return parts


def _epi_conv_a(accs, extra, outs):
    outs[0][...] = accs[0]
    outs[1][...] = accs[1] * accs[2]


def _epi_glu(accs, extra, outs):
    outs[0][...] = accs[0] * jax.nn.sigmoid(accs[1])


def _epi_fox(accs, extra, outs):
    qn_ref, kn_ref = extra
    q_ref, k_ref, kb_ref, v_ref, vb_ref = outs
    for c, part in enumerate(_head_rms(accs[0], qn_ref[...], HD_C)):
        q_ref[:, c * HD_C:(c + 1) * HD_C] = part.astype(BF16)
    for c, part in enumerate(_head_rms(accs[1], kn_ref[...], HD_C)):
        k_ref[:, c * HD_C:(c + 1) * HD_C] = part
        kb_ref[:, c * HD_C:(c + 1) * HD_C] = part.astype(BF16)
    v_ref[...] = accs[2]
    vb_ref[...] = accs[2].astype(BF16)


def _proj_fox_prompt_body(h_ref, wq_ref, wk_ref, wv_ref, qn_ref, kn_ref, q_ref, kb_ref, vb_ref, k4_ref, v4_ref):
    h = h_ref[...]
    acc_v = _dot(h, wv_ref[...])
    for c, part in enumerate(_head_rms(_dot(h, wq_ref[...]), qn_ref[...], HD_C)):
        q_ref[:, c * HD_C:(c + 1) * HD_C] = part.astype(BF16)
    for c, part in enumerate(_head_rms(_dot(h, wk_ref[...]), kn_ref[...], HD_C)):
        k4_ref[:, c, :] = part
        kb_ref[:, c * HD_C:(c + 1) * HD_C] = part.astype(BF16)
    vb_ref[...] = acc_v.astype(BF16)
    for c in range(H_C):
        v4_ref[:, c, :] = acc_v[:, c * HD_C:(c + 1) * HD_C]


def _proj_fox_prompt(layer, h, w, offs, qn, kn, rows):
    d = h.shape[1]
    once = pl.Buffered(1)
    w_specs = [pl.BlockSpec((None, d, D_BR), functools.partial(lambda blk, i: (layer, 0, blk), off // D_BR),
                            pipeline_mode=once) for off in offs]
    fixed = lambda i: (0, 0)
    flat = pl.BlockSpec((TM_FOX, D_BR), lambda i: (i, 0))
    split = pl.BlockSpec((TM_FOX, H_C, HD_C), lambda i: (i, 0, 0))
    return pl.pallas_call(
        _proj_fox_prompt_body, grid=(rows // TM_FOX,),
        in_specs=[pl.BlockSpec((TM_FOX, d), lambda i: (i, 0))] + w_specs
        + [pl.BlockSpec((1, HD_C), fixed), pl.BlockSpec((1, HD_C), fixed)],
        out_specs=[flat, flat, flat, split, split],
        out_shape=[jax.ShapeDtypeStruct((rows, D_BR), BF16)] * 3
        + [jax.ShapeDtypeStruct((rows, H_C, HD_C), F32)] * 2,
        compiler_params=_params("parallel"), name="proj_fox_prompt",
    )(h, w, w, w, qn, kn)


def _epi_forget(accs, extra, outs):
    outs[0][...] = _log_sigmoid(accs[0] + extra[0][...])


def _epi_qmem(accs, extra, outs):
    for c, part in enumerate(_head_rms(accs[0], extra[0][...], HD_M)):
        outs[0][:, c * HD_M:(c + 1) * HD_M] = part.astype(BF16)


def _epi_memkv(accs, extra, outs):
    for c, part in enumerate(_head_rms(accs[0], extra[0][...], HD_M)):
        outs[0][:, c * HD_M:(c + 1) * HD_M] = part
    outs[1][...] = accs[1]


def _rms_cast_body(x_ref, g_ref, o_ref):
    o_ref[...] = _rms_rows(x_ref[...], g_ref[...]).astype(BF16)


def _rms_cast(x, g, tm):
    rows, d = x.shape
    return pl.pallas_call(
        _rms_cast_body, grid=(rows // tm,),
        in_specs=[pl.BlockSpec((tm, d), lambda i: (i, 0)), pl.BlockSpec((1, d), lambda i: (0, 0))],
        out_specs=pl.BlockSpec((tm, d), lambda i: (i, 0)),
        out_shape=jax.ShapeDtypeStruct((rows, d), BF16),
        compiler_params=_params("parallel"), name="rms_cast",
    )(x, g.reshape(1, d))


def _ln_silu(x, g, b):
    mu = jnp.mean(x, axis=-1, keepdims=True)
    xc = x - mu
    var = jnp.mean(xc * xc, axis=-1, keepdims=True)
    z = xc * lax.rsqrt(var + LN_EPS) * g + b
    return z * jax.nn.sigmoid(z)


def _conv_prompt_body(tiles_per_seq, ba_ref, cx_ref, cxh_ref, gl_ref, glh_ref, wa_ref, wb_ref,
                      lg_ref, lb_ref, za_ref, zb_ref, sa, sb):
    first = (pl.program_id(0) % tiles_per_seq) == 0
    sa[0:HALO, :] = jnp.where(first, 0.0, cxh_ref[...])
    sa[HALO:, :] = cx_ref[...]
    sb[0, 0:HALO, :] = jnp.where(first, 0.0, glh_ref[...])
    sb[0, HALO:, :] = gl_ref[...]
    acc = None
    for k in range(K_A):
        term = wa_ref[k:k + 1, :] * sa[pl.ds(HALO - (K_A - 1) + k, TC), :]
        acc = term if acc is None else acc + term
    za_ref[...] = (ba_ref[...] * acc).astype(BF16)

    n_shift = TC + HALO - SUBLANES
    for s in range(1, SUBLANES):
        sb[s, 0:n_shift, :] = sb[0, pl.ds(s, n_shift), :]

    def chunk(ci, carry):
        base = pl.multiple_of(ci * CONV_ROWS, CONV_ROWS)
        acc = None
        for k in range(K_B):
            whole, s = divmod(HALO - (K_B - 1) + k, SUBLANES)
            rows = sb[s, pl.ds(base + whole * SUBLANES, CONV_ROWS), :]
            term = wb_ref[k][None] * rows.reshape(CONV_ROWS // SUBLANES, SUBLANES, rows.shape[1])
            acc = term if acc is None else acc + term
        acc = acc.reshape(CONV_ROWS, acc.shape[2])
        zb_ref[pl.ds(base, CONV_ROWS), :] = _ln_silu(acc, lg_ref[...], lb_ref[...]).astype(BF16)
        return carry

    lax.fori_loop(0, TC // CONV_ROWS, chunk, 0)


def _conv_prompt(b_a, cx, glu, n_rows, seq, wa, wb, ln_g, ln_b):
    c = cx.shape[1]
    tile = lambda i: (i, 0)
    halo = lambda i: (jnp.maximum(i * (TC // HALO) - 1, 0), 0)
    fixed = lambda i: (0, 0)
    return pl.pallas_call(
        functools.partial(_conv_prompt_body, seq // TC),
        grid=(n_rows // TC,),
        in_specs=[pl.BlockSpec((TC, c), tile), pl.BlockSpec((TC, c), tile), pl.BlockSpec((HALO, c), halo),
                  pl.BlockSpec((TC, c), tile), pl.BlockSpec((HALO, c), halo),
                  pl.BlockSpec((K_A, c), fixed), pl.BlockSpec((K_B, SUBLANES, c), lambda i: (0, 0, 0)),
                  pl.BlockSpec((1, c), fixed), pl.BlockSpec((1, c), fixed)],
        out_specs=[pl.BlockSpec((TC, c), tile), pl.BlockSpec((TC, c), tile)],
        out_shape=[jax.ShapeDtypeStruct((n_rows, c), BF16)] * 2,
        scratch_shapes=[pltpu.VMEM((TC + HALO, c), F32), pltpu.VMEM((SUBLANES, TC + HALO, c), F32)],
        compiler_params=_params("parallel"), name="conv_prompt",
    )(b_a, cx, cx, glu, glu, wa, jnp.broadcast_to(wb[:, None, :], (K_B, SUBLANES, c)),
      ln_g.reshape(1, c), ln_b.reshape(1, c))


def _conv_sample_body(n_t, ba_ref, ea_ref, eb_ref, wa_ref, wb_ref, lg_ref, lb_ref, za_ref, zb_ref):
    for t in range(n_t):
        acc = None
        for k in range(K_A):
            term = wa_ref[k:k + 1, :] * ea_ref[t + k]
            acc = term if acc is None else acc + term
        za_ref[t] = (ba_ref[t] * acc).astype(BF16)
        acc = None
        for k in range(K_B):
            term = wb_ref[k:k + 1, :] * eb_ref[t + k]
            acc = term if acc is None else acc + term
        zb_ref[t] = _ln_silu(acc, lg_ref[...], lb_ref[...]).astype(BF16)


def _conv_sample(ba_t, ext_a_t, ext_b_t, wa, wb, ln_g, ln_b):
    n_t, nb, c = ba_t.shape
    return pl.pallas_call(
        functools.partial(_conv_sample_body, n_t),
        out_shape=[jax.ShapeDtypeStruct((n_t, nb, c), BF16)] * 2,
        compiler_params=pltpu.CompilerParams(vmem_limit_bytes=VMEM_LIMIT), name="conv_sample",
    )(ba_t, ext_a_t, ext_b_t, wa, wb, ln_g.reshape(1, c), ln_b.reshape(1, c))


def _cumsum_body(x_ref, o_ref, carry):
    @pl.when(pl.program_id(1) == 0)
    def _():
        carry[...] = jnp.zeros_like(carry)

    r = lax.broadcasted_iota(jnp.int32, (CS, CS), 0)
    c = lax.broadcasted_iota(jnp.int32, (CS, CS), 1)
    tri = jnp.where(c <= r, 1.0, 0.0).astype(F32)
    out = jnp.dot(tri, x_ref[...], preferred_element_type=F32, precision=lax.Precision.HIGHEST) + carry[...]
    o_ref[...] = out
    carry[...] = out[CS - 1:CS, :]


def _cumsum_rows(x):
    nb, rows, c = x.shape
    return pl.pallas_call(
        _cumsum_body, grid=(nb, rows // CS),
        in_specs=[pl.BlockSpec((None, CS, c), lambda b, i: (b, i, 0))],
        out_specs=pl.BlockSpec((None, CS, c), lambda b, i: (b, i, 0)),
        out_shape=jax.ShapeDtypeStruct((nb, rows, c), F32),
        scratch_shapes=[pltpu.VMEM((1, c), F32)],
        compiler_params=_params("parallel", "arbitrary"), name="cumsum",
    )(x)


def _fox_prompt_body(qt_ref, kt_ref, q_ref, k_ref, v_ref, cq_ref, ck_ref, o_ref, m_sc, l_sc, acc_sc, cq_sc):
    h = pl.program_id(1)
    i = qt_ref[pl.program_id(2)]
    j = kt_ref[pl.program_id(2)]

    @pl.when(j == 0)
    def _():
        m_sc[...] = jnp.full_like(m_sc, NEG)
        l_sc[...] = jnp.zeros_like(l_sc)
        acc_sc[...] = jnp.zeros_like(acc_sc)
        lane = lax.broadcasted_iota(jnp.int32, cq_ref.shape, 1)
        cq_sc[...] = jnp.sum(jnp.where(lane == h, cq_ref[...], 0.0), axis=1, keepdims=True)

    def update(on_diagonal):
        for rb in range(TQ // FOX_ROWS):
            rows = slice(rb * FOX_ROWS, (rb + 1) * FOX_ROWS)
            n_keys = (rb + 1) * FOX_ROWS if on_diagonal else TK
            t = _dot_nt(q_ref[rows, :], k_ref[0:n_keys, :]) * (HD_C ** -0.5 * LOG2E) - LOG2E * ck_ref[:, 0:n_keys]
            if on_diagonal:
                row = rb * FOX_ROWS + lax.broadcasted_iota(jnp.int32, (FOX_ROWS, n_keys), 0)
                col = lax.broadcasted_iota(jnp.int32, (FOX_ROWS, n_keys), 1)
                t = jnp.where(col <= row, t, -jnp.inf)
            cq = LOG2E * cq_sc[rows, :]
            m_old = m_sc[rows, :]
            m_new = jnp.maximum(m_old, jnp.max(t, axis=1, keepdims=True) + cq)
            alpha = jnp.exp2(m_old - m_new)
            p = jnp.exp2(t - (m_new - cq))
            l_sc[rows, :] = alpha * l_sc[rows, :] + jnp.sum(p, axis=1, keepdims=True)
            acc_sc[rows, :] = alpha * acc_sc[rows, :] + _dot(p.astype(BF16), v_ref[0:n_keys, :])
            m_sc[rows, :] = m_new

    @pl.when(j < i)
    def _():
        update(False)

    @pl.when(j == i)
    def _():
        update(True)
        o_ref[...] = (acc_sc[...] / l_sc[...]).astype(BF16)


def _fox_prompt(qb, kb, vb, c_tok, c_row, nb, seq):
    assert TQ == TK
    nt = seq // TQ
    pairs = [(i, j) for i in range(nt) for j in range(i + 1)]
    q_tiles = jnp.array([p[0] for p in pairs], jnp.int32)
    k_tiles = jnp.array([p[1] for p in pairs], jnp.int32)
    q_map = lambda b, h, p, qt, kt: (b * nt + qt[p], h)
    k_map = lambda b, h, p, qt, kt: (b * nt + kt[p], h)
    grid_spec = pltpu.PrefetchScalarGridSpec(
        num_scalar_prefetch=2, grid=(nb, H_C, len(pairs)),
        in_specs=[pl.BlockSpec((TQ, HD_C), q_map), pl.BlockSpec((TK, HD_C), k_map), pl.BlockSpec((TK, HD_C), k_map),
                  pl.BlockSpec((TQ, LANES), lambda b, h, p, qt, kt: (b * nt + qt[p], 0)),
                  pl.BlockSpec((None, None, 1, TK), lambda b, h, p, qt, kt: (b, h, 0, kt[p]))],
        out_specs=pl.BlockSpec((TQ, HD_C), q_map),
        scratch_shapes=[pltpu.VMEM((TQ, 1), F32), pltpu.VMEM((TQ, 1), F32), pltpu.VMEM((TQ, HD_C), F32),
                        pltpu.VMEM((TQ, 1), F32)])
    return pl.pallas_call(
        _fox_prompt_body, grid_spec=grid_spec,
        out_shape=jax.ShapeDtypeStruct((nb * seq, H_C * HD_C), BF16),
        compiler_params=_params("parallel", "parallel", "arbitrary"), name="fox_prompt",
    )(q_tiles, k_tiles, qb, kb, vb, c_tok, c_row)


def _fox_sample_body(n_steps, pt_ref, q_ref, kn_ref, vn_ref, lfr_ref, lfc_ref, *rest):
    g_pages = PAGES_PER_STEP
    k_refs = rest[:g_pages]
    v_refs = rest[g_pages:2 * g_pages]
    lf_refs = rest[2 * g_pages:3 * g_pages]
    o_ref, m_sc, l_sc, acc_sc, cq_sc, run_sc, lf_sc = rest[3 * g_pages:]
    step = pl.program_id(1)
    n_q = q_ref.shape[0]
    n_key = PAGE * H_C
    scale = HD_C ** -0.5
    q = q_ref[...]

    @pl.when(step == 0)
    def _():
        r = lax.broadcasted_iota(jnp.int32, (n_q, n_q), 0)
        c = lax.broadcasted_iota(jnp.int32, (n_q, n_q), 1)
        ok = ((r % H_C) == (c % H_C)) & (c // H_C <= r // H_C)
        ok_t = ((r % H_C) == (c % H_C)) & (r // H_C <= c // H_C)
        cq_col = jnp.sum(jnp.where(ok, lfr_ref[...], 0.0), axis=1, keepdims=True)
        cq_row = jnp.sum(jnp.where(ok_t, lfc_ref[...], 0.0), axis=0, keepdims=True)
        s = _dot_nt(q, kn_ref[...]) * scale + (cq_col - cq_row)
        s = jnp.where(ok, s, -jnp.inf)
        m = jnp.max(s, axis=1, keepdims=True)
        p = jnp.exp(s - m)
        m_sc[...] = m
        l_sc[...] = jnp.sum(p, axis=1, keepdims=True)
        acc_sc[...] = _dot(p.astype(BF16), vn_ref[...])
        cq_sc[...] = cq_col
        run_sc[...] = jnp.zeros_like(run_sc)

    for g in range(g_pages):
        lf_sc[g:g + 1, :] = lf_refs[g][...]
    lf = lf_sc[...]
    lane = lax.broadcasted_iota(jnp.int32, lf.shape, 1)
    incl = lf
    sh = H_C
    while sh < n_key:
        incl = incl + jnp.where(lane < n_key - sh, pltpu.roll(incl, n_key - sh, axis=1), 0.0)
        sh *= 2
    tot = jnp.where(lane < H_C, incl, 0.0)
    sh = H_C
    while sh < n_key:
        tot = tot + pltpu.roll(tot, sh, axis=1)
        sh *= 2
    after = incl - lf

    r = lax.broadcasted_iota(jnp.int32, (n_q, n_key), 0)
    c = lax.broadcasted_iota(jnp.int32, (n_q, n_key), 1)
    cq_masked = jnp.where((r % H_C) == (c % H_C), cq_sc[...], -jnp.inf)
    run = run_sc[...]
    scores = []
    for g in range(g_pages):
        k2 = k_refs[g][...].reshape(n_key, HD_C).astype(BF16)
        scores.append(_dot_nt(q, k2) * scale + (cq_masked + (after[g:g + 1, :] + run)))
        run = run + tot[g:g + 1, :]
    run_sc[...] = run

    top = scores[0]
    for s in scores[1:]:
        top = jnp.maximum(top, s)
    m_old = m_sc[...]
    m_new = jnp.maximum(m_old, jnp.max(top, axis=1, keepdims=True))
    alpha = jnp.exp(m_old - m_new)
    p_sum = None
    pv = None
    for g in range(g_pages):
        p = jnp.exp(scores[g] - m_new)
        v2 = v_refs[g][...].reshape(n_key, HD_C).astype(BF16)
        term = _dot(p.astype(BF16), v2)
        p_sum = p if p_sum is None else p_sum + p
        pv = term if pv is None else pv + term
    l_new = alpha * l_sc[...] + jnp.sum(p_sum, axis=1, keepdims=True)
    acc_new = alpha * acc_sc[...] + pv
    m_sc[...] = m_new
    l_sc[...] = l_new
    acc_sc[...] = acc_new

    @pl.when(step == n_steps - 1)
    def _():
        o_ref[...] = (acc_new / l_new).astype(BF16)


def _fox_sample(layer, page_table, q, k_new, v_new, lf_row, lf_col, cache_k, cache_v, cache_lf):
    nb, n_q, _ = q.shape
    n_pages = page_table.shape[1]
    g_pages = PAGES_PER_STEP
    n_steps = n_pages // g_pages
    pt = page_table.reshape(-1)

    def page(g, b, s, pt_ref):
        return pt_ref[b * n_pages + (n_pages - 1 - (s * g_pages + g))]

    per_b = lambda b, s, pt_ref: (b, 0, 0)
    kv_specs = [pl.BlockSpec((None, None, PAGE, H_C, HD_C),
                             functools.partial(lambda g, b, s, pt_ref: (layer, page(g, b, s, pt_ref), 0, 0, 0), g))
                for g in range(g_pages)]
    lf_specs = [pl.BlockSpec((None, None, 1, PAGE * H_C),
                             functools.partial(lambda g, b, s, pt_ref: (layer, page(g, b, s, pt_ref), 0, 0), g))
                for g in range(g_pages)]
    grid_spec = pltpu.PrefetchScalarGridSpec(
        num_scalar_prefetch=1, grid=(nb, n_steps),
        in_specs=[pl.BlockSpec((None, n_q, HD_C), per_b), pl.BlockSpec((None, n_q, HD_C), per_b),
                  pl.BlockSpec((None, n_q, HD_C), per_b), pl.BlockSpec((None, 1, n_q), per_b),
                  pl.BlockSpec((None, n_q, 1), per_b)] + kv_specs + kv_specs + lf_specs,
        out_specs=pl.BlockSpec((None, n_q, HD_C), per_b),
        scratch_shapes=[pltpu.VMEM((n_q, 1), F32), pltpu.VMEM((n_q, 1), F32), pltpu.VMEM((n_q, HD_C), F32),
                        pltpu.VMEM((n_q, 1), F32), pltpu.VMEM((1, PAGE * H_C), F32),
                        pltpu.VMEM((g_pages, PAGE * H_C), F32)])
    return pl.pallas_call(
        functools.partial(_fox_sample_body, n_steps), grid_spec=grid_spec,
        out_shape=jax.ShapeDtypeStruct((nb, n_q, HD_C), BF16),
        compiler_params=_params("parallel", "arbitrary"), name="fox_sample",
    )(pt, q, k_new, v_new, lf_row, lf_col, *([cache_k] * g_pages), *([cache_v] * g_pages),
      *([cache_lf] * g_pages))


def _mem_attn_body(q_ref, mk_ref, mv_ref, o_ref):
    q = q_ref[...].astype(BF16)
    for h in range(H_M):
        cols = slice(h * HD_M, (h + 1) * HD_M)
        s = _dot_nt(q[:, cols], mk_ref[:, cols].astype(BF16)) * (HD_M ** -0.5)
        m = jnp.max(s, axis=1, keepdims=True)
        p = jnp.exp(s - m)
        l = jnp.sum(p, axis=1, keepdims=True)
        o = _dot(p.astype(BF16), mv_ref[:, cols].astype(BF16)) / l
        o_ref[:, cols] = o.astype(o_ref.dtype)


def _mem_attn(layer, q, mk, mv, nb, rows_per_b, tq, out_dtype):
    d = q.shape[1]
    nt = rows_per_b // tq
    mem_spec = pl.BlockSpec((None, None, N_MEM, d), lambda b, i: (layer, b, 0, 0))
    return pl.pallas_call(
        _mem_attn_body, grid=(nb, nt),
        in_specs=[pl.BlockSpec((tq, d), lambda b, i: (b * nt + i, 0)), mem_spec, mem_spec],
        out_specs=pl.BlockSpec((tq, d), lambda b, i: (b * nt + i, 0)),
        out_shape=jax.ShapeDtypeStruct((nb * rows_per_b, d), out_dtype),
        compiler_params=_params("parallel", "arbitrary"), name="mem_attn",
    )(q, mk, mv)


def _merge_body(x_ref, h_ref, *refs):
    z_refs = refs[:N_BRANCH]
    wg_refs = refs[N_BRANCH:2 * N_BRANCH]
    w_ref, wo_ref, o_ref = refs[2 * N_BRANCH:]

    @pl.when(pl.program_id(1) == 0)
    def _():
        o_ref[...] = x_ref[...]

    h = h_ref[...]
    merged = None
    for br, (z, wg) in enumerate(zip(z_refs, wg_refs)):
        term = jax.nn.sigmoid(_dot(h, wg[...])) * _dot(z[...], w_ref[br])
        merged = term if merged is None else merged + term
    merged = merged.astype(BF16)
    for c in range(o_ref.shape[1] // DOWN_CHUNK):
        cols = slice(c * DOWN_CHUNK, (c + 1) * DOWN_CHUNK)
        o_ref[:, cols] += _dot(merged, wo_ref[:, cols])


def _merge(layer, x, h, zs, w_all, off_gates, w_outs, w_o):
    rows, d = x.shape
    tn = TN_MERGE
    row = lambda i, n: (i, 0)
    once = pl.Buffered(1)
    in_specs = [pl.BlockSpec((TM_MERGE, d), row, pipeline_mode=once), pl.BlockSpec((TM_MERGE, d), row)]
    in_specs += [pl.BlockSpec((TM_MERGE, D_BR), row, pipeline_mode=once) for _ in range(N_BRANCH)]
    in_specs += [pl.BlockSpec((None, d, tn),
                              functools.partial(lambda blk, i, n: (layer, 0, blk + n), (off_gates + br * d) // tn))
                 for br in range(N_BRANCH)]
    in_specs += [pl.BlockSpec((None, N_BRANCH, D_BR, tn), lambda i, n: (layer, 0, 0, n)),
                 pl.BlockSpec((None, tn, d), lambda i, n: (layer, n, 0))]
    return pl.pallas_call(
        _merge_body, grid=(rows // TM_MERGE, d // tn), in_specs=in_specs,
        out_specs=pl.BlockSpec((TM_MERGE, d), row),
        out_shape=jax.ShapeDtypeStruct((rows, d), F32),
        compiler_params=_params("parallel", "arbitrary"), name="merge",
    )(x, h, *zs, *([w_all] * N_BRANCH), w_outs, w_o)


def kernel(x_prompt, x_sample, cache_fox_k, cache_fox_v, cache_fox_lf, cache_mem_k, cache_mem_v, state_conv_a, state_conv_b, page_table, mem_prompt, n_ffn1, w_ffn1_gate, w_ffn1_up, w_ffn1_down, n_mix, w_in, b_forget, conv_a, w_a_out, conv_b, ln_b_g, ln_b_b, w_b_out, qn_c, kn_c, w_c_out, n_mem, w_mem_k, w_mem_v, qn_m, kn_m, w_m_out, w_o, n_ffn2, w_ffn2_gate, w_ffn2_up, w_ffn2_down):
    depth = w_in.shape[0]
    bp, seq, d = x_prompt.shape
    db, dseq, _ = x_sample.shape
    n_p = bp * seq
    n_s = db * dseq
    n_phys = cache_fox_lf.shape[1]

    x = jnp.concatenate([x_prompt.reshape(n_p, d), x_sample.reshape(n_s, d)], axis=0)
    mem_rows = mem_prompt.reshape(bp * N_MEM, d)
    cache_lf_rows = cache_fox_lf.reshape(depth, n_phys, 1, PAGE * H_C)
    cache_mk = cache_mem_k.reshape(depth, db, N_MEM, D_BR)
    cache_mv = cache_mem_v.reshape(depth, db, N_MEM, D_BR)
    w_branch = jnp.stack([w_a_out, w_b_out, w_c_out, w_m_out], axis=1).astype(BF16)
    w_o_b = w_o.astype(BF16)

    fox_end = 8 * D_BR
    off_ba, off_ca, off_xa, off_g1, off_g2, off_q, off_k, off_v = [s * D_BR for s in range(8)]
    off_qm = fox_end
    off_gates = fox_end + D_BR

    w_t = jnp.swapaxes(w_in, 1, 2)
    w_all = _w_prep(w_t, fox_end)
    w_mkv = jnp.concatenate([w_mem_k, w_mem_v], axis=2).astype(BF16)

    outs = [[] for _ in range(12)]
    for l in range(depth):
        b_f = jnp.pad(b_forget[l], (0, LANES - H_C)).reshape(1, LANES)

        x, h = _ffn(l, x, n_ffn1[l], w_ffn1_gate, w_ffn1_up, w_ffn1_down, n_mix[l])

        b_a, cx = _proj(l, h, w_all, [off_ba, off_ca, off_xa], D_BR, [], _epi_conv_a, [F32, F32],
                        name="proj_conv_a")
        (glu,) = _proj(l, h, w_all, [off_g1, off_g2], D_BR, [], _epi_glu, [F32], name="proj_glu")
        fox_norms = [qn_c[l].reshape(1, HD_C), kn_c[l].reshape(1, HD_C)]
        q_c, kb_c, vb_c, k_p, v_p = _proj_fox_prompt(l, h, w_all, [off_q, off_k, off_v], *fox_norms, n_p)
        q_s, k_s, kb_s, v_s, vb_s = _proj(
            l, h, w_all, [off_q, off_k, off_v], D_BR, fox_norms, _epi_fox, [BF16, F32, BF16, F32, BF16],
            tm=n_s, name="proj_fox_sample", row0=n_p, rows=n_s)
        (lf_pad,) = _proj(l, h, w_t, [fox_end], LANES, [b_f], _epi_forget, [F32], tn=LANES, name="proj_forget",
                          transposed=True)
        (q_m,) = _proj(l, h, w_all, [off_qm], D_BR, [qn_m[l].reshape(1, HD_M)], _epi_qmem, [BF16],
                       name="proj_qmem")

        m_h = _rms_cast(mem_rows, n_mem[l], N_MEM)
        mk_p, mv_p = _proj(l, m_h, w_mkv, [0, D_BR], D_BR, [kn_m[l].reshape(1, HD_M)], _epi_memkv, [F32, F32],
                           tm=N_MEM, name="proj_memkv")

        za_p, zb_p = _conv_prompt(b_a, cx, glu, n_p, seq, conv_a[l], conv_b[l], ln_b_g[l], ln_b_b[l])
        cx_s = cx[n_p:].reshape(db, dseq, D_BR)
        glu_s = glu[n_p:].reshape(db, dseq, D_BR)
        ext_a = jnp.concatenate([state_conv_a[l], cx_s], axis=1)
        ext_b = jnp.concatenate([state_conv_b[l], glu_s], axis=1)
        ba_t = jnp.swapaxes(b_a[n_p:].reshape(db, dseq, D_BR), 0, 1)
        za_s, zb_s = _conv_sample(ba_t, jnp.swapaxes(ext_a, 0, 1), jnp.swapaxes(ext_b, 0, 1),
                                  conv_a[l], conv_b[l], ln_b_g[l], ln_b_b[l])
        z_a = jnp.concatenate([za_p, jnp.swapaxes(za_s, 0, 1).reshape(n_s, D_BR)], axis=0)
        z_b = jnp.concatenate([zb_p, jnp.swapaxes(zb_s, 0, 1).reshape(n_s, D_BR)], axis=0)

        lf = lf_pad[:, :H_C]
        c_tok = _cumsum_rows(lf_pad[:n_p].reshape(bp, seq, LANES))
        c_row = jnp.swapaxes(c_tok[:, :, :H_C], 1, 2).reshape(bp, H_C, 1, seq)
        yc_p = _fox_prompt(q_c, kb_c, vb_c, c_tok.reshape(n_p, LANES), c_row, bp, seq)
        lf_s = lf[n_p:].reshape(db, dseq * H_C)
        yc_s = _fox_sample(
            l, page_table, q_s.reshape(db, dseq * H_C, HD_C), kb_s.reshape(db, dseq * H_C, HD_C),
            vb_s.reshape(db, dseq * H_C, HD_C), lf_s.reshape(db, 1, dseq * H_C),
            lf_s.reshape(db, dseq * H_C, 1), cache_fox_k, cache_fox_v, cache_lf_rows)
        y_c = jnp.concatenate([yc_p, yc_s.reshape(n_s, D_BR)], axis=0)

        ym_p = _mem_attn(0, q_m, mk_p.reshape(1, bp, N_MEM, D_BR), mv_p.reshape(1, bp, N_MEM, D_BR),
                         bp, seq, TQ_M, BF16)
        qm_s = jnp.pad(q_m[n_p:].reshape(db, dseq, D_BR).astype(F32), ((0, 0), (0, SUBLANES - dseq), (0, 0)))
        ym_s = _mem_attn(l, qm_s.reshape(db * SUBLANES, D_BR), cache_mk, cache_mv, db, SUBLANES, SUBLANES, F32)
        ym_s = ym_s.reshape(db, SUBLANES, D_BR)[:, :dseq].reshape(n_s, D_BR)
        y_m = jnp.concatenate([ym_p, ym_s.astype(BF16)], axis=0)

        x = _merge(l, x, h, [z_a, z_b, y_c, y_m], w_all, off_gates, w_branch, w_o_b)
        x = _ffn(l, x, n_ffn2[l], w_ffn2_gate, w_ffn2_up, w_ffn2_down)

        per_layer = [
            k_p.reshape(bp, seq, H_C, HD_C), v_p.reshape(bp, seq, H_C, HD_C),
            lf[:n_p].reshape(bp, seq, H_C),
            k_s.reshape(db, dseq, H_C, HD_C), v_s.reshape(db, dseq, H_C, HD_C),
            lf[n_p:].reshape(db, dseq, H_C),
            mk_p.reshape(bp, N_MEM, H_M, HD_M), mv_p.reshape(bp, N_MEM, H_M, HD_M),
            cx[:n_p].reshape(bp, seq, D_BR)[:, seq - (K_A - 1):], ext_a[:, dseq:],
            glu[:n_p].reshape(bp, seq, D_BR)[:, seq - (K_B - 1):], ext_b[:, dseq:],
        ]
        for acc, val in zip(outs, per_layer):
            acc.append(val)

    return (x[:n_p].reshape(bp, seq, d), x[n_p:].reshape(db, dseq, d)) + tuple(jnp.stack(o) for o in outs)
```
